```python
import jax, jax.numpy as jnp
from jax import lax
import numpy as np

D_MODEL = 2048
BATCH = 1
SEQ = 16384
DEPTH = 1

N_HEADS = 16
QK_NOPE_DIM = 128
QK_ROPE_DIM = 64
QK_HEAD_DIM = QK_NOPE_DIM + QK_ROPE_DIM
V_HEAD_DIM = 128
Q_LORA_RANK = 512
KV_LORA_RANK = 512
MLA_WIDTH = N_HEADS * V_HEAD_DIM
ROPE_THETA = 10000.0
Q_BLOCK = 128
POOL_WINDOWS = (2, 4, 8, 16)
POOL_GROUPS = len(POOL_WINDOWS)
POOL_GROUP_DIM = 256
POOL_WIDTH = POOL_GROUPS * POOL_GROUP_DIM
N_BRANCHES = 2
RMS_EPS = 1e-6
LN_EPS = 1e-5
DEEPNORM_ALPHA = (2.0 * DEPTH) ** 0.25
DEEPNORM_BETA = (8.0 * DEPTH) ** -0.25

IN_SPLITS = (Q_LORA_RANK, KV_LORA_RANK + QK_ROPE_DIM, MLA_WIDTH, POOL_WIDTH, POOL_WIDTH, N_BRANCHES * D_MODEL)
IN_WIDTH = sum(IN_SPLITS)
IN_OFFSETS = tuple(sum(IN_SPLITS[:i + 1]) for i in range(len(IN_SPLITS) - 1))

kernel_name = 'hybrid_mla_pool_gated_deepnorm_adaln'


def layer_norm(x, g=None, b=None):
    xf = x.astype(jnp.float32)
    mu = jnp.mean(xf, axis=-1, keepdims=True)
    var = jnp.mean(jnp.square(xf - mu), axis=-1, keepdims=True)
    y = (xf - mu) * lax.rsqrt(var + LN_EPS)
    if g is not None:
        y = y * g.astype(jnp.float32) + b.astype(jnp.float32)
    return y.astype(x.dtype)


def rms_norm(x, g):
    xf = x.astype(jnp.float32)
    y = xf * lax.rsqrt(jnp.mean(jnp.square(xf), axis=-1, keepdims=True) + RMS_EPS)
    return (y * g.astype(jnp.float32)).astype(x.dtype)


def rope_tables(positions, dtype):
    inv_freq = ROPE_THETA ** (-jnp.arange(0, QK_ROPE_DIM, 2, dtype=jnp.float32) / QK_ROPE_DIM)
    ang = positions.astype(jnp.float32)[..., None] * inv_freq
    return jnp.cos(ang)[:, :, None, :].astype(dtype), jnp.sin(ang)[:, :, None, :].astype(dtype)


def apply_rope(x, cos, sin):
    half = QK_ROPE_DIM // 2
    x1, x2 = x[..., :half], x[..., half:]
    return jnp.concatenate([x1 * cos - x2 * sin, x1 * sin + x2 * cos], axis=-1)


def causal_attention(q, k, v):
    B, S, H, Dqk = q.shape
    Dv = v.shape[-1]
    nb = S // Q_BLOCK
    scale = Dqk ** -0.5
    q_blocks = q.reshape(B, nb, Q_BLOCK, H, Dqk).transpose(1, 0, 2, 3, 4)
    starts = jnp.arange(nb, dtype=jnp.int32) * Q_BLOCK
    k_idx = jnp.arange(S, dtype=jnp.int32)

    def one_block(args):
        q_blk, start = args
        s = jnp.einsum('bqhd,bkhd->bhqk', q_blk, k, preferred_element_type=jnp.float32) * scale
        q_idx = start + jnp.arange(Q_BLOCK, dtype=jnp.int32)
        causal = k_idx[None, :] <= q_idx[:, None]
        s = jnp.where(causal[None, None], s, -jnp.inf)
        p = jax.nn.softmax(s, axis=-1).astype(v.dtype)
        return jnp.einsum('bhqk,bkhd->bqhd', p, v)

    out = lax.map(one_block, (q_blocks, starts))
    return out.transpose(1, 0, 2, 3, 4).reshape(B, S, H * Dv)


def mla_mixer(q_lat, kv_lat, positions, q_norm_g, w_q_b, kv_norm_g, w_kv_b):
    B, S, _ = q_lat.shape
    q = jnp.einsum('bsr,re->bse', rms_norm(q_lat, q_norm_g), w_q_b).reshape(B, S, N_HEADS, QK_HEAD_DIM)
    q_nope, q_pe = q[..., :QK_NOPE_DIM], q[..., QK_NOPE_DIM:]
    c_kv, k_pe = kv_lat[..., :KV_LORA_RANK], kv_lat[..., KV_LORA_RANK:]
    kv = jnp.einsum('bsr,re->bse', rms_norm(c_kv, kv_norm_g), w_kv_b).reshape(B, S, N_HEADS, QK_NOPE_DIM + V_HEAD_DIM)
    k_nope, v = kv[..., :QK_NOPE_DIM], kv[..., QK_NOPE_DIM:]
    cos, sin = rope_tables(positions, q.dtype)
    q_pe = apply_rope(q_pe, cos, sin)
    k_pe = apply_rope(k_pe[:, :, None, :], cos, sin)
    q_full = jnp.concatenate([q_nope, q_pe], axis=-1)
    k_full = jnp.concatenate([k_nope, jnp.broadcast_to(k_pe, (B, S, N_HEADS, QK_ROPE_DIM))], axis=-1)
    return causal_attention(q_full, k_full, v)


def pool_mixer(u, w_pool_g, pool_scale):
    B, S, _ = u.shape
    uf = u.astype(jnp.float32)
    cs = jnp.concatenate([jnp.zeros((B, 1, POOL_WIDTH), jnp.float32), lax.cumsum(uf, axis=1)], axis=1)
    t = jnp.arange(S, dtype=jnp.int32)
    pooled = []
    for g, w in enumerate(POOL_WINDOWS):
        sl = slice(g * POOL_GROUP_DIM, (g + 1) * POOL_GROUP_DIM)
        lo = jnp.maximum(t + 1 - w, 0)
        win_sum = cs[:, 1:, sl] - jnp.take(cs[:, :, sl], lo, axis=1)
        count = jnp.minimum(t + 1, w).astype(jnp.float32)[None, :, None]
        pooled.append(win_sum / count - uf[..., sl])
    pooled = jnp.stack(pooled, axis=2).astype(u.dtype)
    mixed = jnp.einsum('bsgi,gio->bsgo', pooled, w_pool_g).reshape(B, S, POOL_WIDTH)
    return mixed * pool_scale


def setup_inputs(seed: int = 0) -> dict:
    key = jax.random.key(seed)
    ks = jax.random.split(key, 18)
    L, D = DEPTH, D_MODEL

    def nrm(k, shape, scale):
        return jax.random.normal(k, shape, jnp.float32) * scale

    return {
        'x': nrm(ks[0], (BATCH, SEQ, D), 1.0),
        'c': nrm(ks[1], (BATCH, D), 1.0),
        'positions': jnp.broadcast_to(jnp.arange(SEQ, dtype=jnp.int32), (BATCH, SEQ)),
        'w_ada': nrm(ks[2], (L, D, 3 * D), 0.1 * D ** -0.5),
        'b_ada': nrm(ks[3], (L, 3 * D), 0.01),
        'w_in': nrm(ks[4], (L, D, IN_WIDTH), D ** -0.5),
        'b_gates': nrm(ks[5], (L, N_BRANCHES * D), 0.01),
        'q_norm_g': 1.0 + nrm(ks[6], (L, Q_LORA_RANK), 0.01),
        'w_q_b': nrm(ks[7], (L, Q_LORA_RANK, N_HEADS * QK_HEAD_DIM), Q_LORA_RANK ** -0.5),
        'kv_norm_g': 1.0 + nrm(ks[8], (L, KV_LORA_RANK), 0.01),
        'w_kv_b': nrm(ks[9], (L, KV_LORA_RANK, N_HEADS * (QK_NOPE_DIM + V_HEAD_DIM)), KV_LORA_RANK ** -0.5),
        'w_mla_o': nrm(ks[10], (L, MLA_WIDTH, D), DEEPNORM_BETA * MLA_WIDTH ** -0.5),
        'w_pool_g': nrm(ks[11], (L, POOL_GROUPS, POOL_GROUP_DIM, POOL_GROUP_DIM), POOL_GROUP_DIM ** -0.5),
        'pool_scale': 1.0 + nrm(ks[12], (L, POOL_WIDTH), 0.02),
        'w_pool_o': nrm(ks[13], (L, POOL_WIDTH, D), DEEPNORM_BETA * POOL_WIDTH ** -0.5),
        'w_out': nrm(ks[14], (L, D, D), DEEPNORM_BETA * D ** -0.5),
        'ln_g': 1.0 + nrm(ks[15], (L, D), 0.01),
        'ln_b': nrm(ks[16], (L, D), 0.01),
    }


def reference(x, c, positions, w_ada, b_ada, w_in, b_gates, q_norm_g, w_q_b, kv_norm_g, w_kv_b,
              w_mla_o, w_pool_g, pool_scale, w_pool_o, w_out, ln_g, ln_b):
    for l in range(DEPTH):
        mod = jnp.einsum('bd,de->be', jax.nn.silu(c), w_ada[l]) + b_ada[l]
        shift, scale, gate = jnp.split(mod[:, None, :], 3, axis=-1)
        h = layer_norm(x) * (1.0 + scale) + shift
        proj = jnp.einsum('bsd,de->bse', h, w_in[l])
        q_lat, kv_lat, mla_gate, pool_in, pool_gate, gate_logits = jnp.split(proj, IN_OFFSETS, axis=-1)
        y_mla = mla_mixer(q_lat, kv_lat, positions, q_norm_g[l], w_q_b[l], kv_norm_g[l], w_kv_b[l])
        y_mla = jnp.einsum('bse,ed->bsd', y_mla * jax.nn.silu(mla_gate), w_mla_o[l])
        y_pool = pool_mixer(pool_in, w_pool_g[l], pool_scale[l])
        y_pool = jnp.einsum('bse,ed->bsd', y_pool * jax.nn.silu(pool_gate), w_pool_o[l])
        g_mla, g_pool = jnp.split(jax.nn.sigmoid(gate_logits + b_gates[l]), N_BRANCHES, axis=-1)
        y = jnp.einsum('bsd,de->bse', g_mla * y_mla + g_pool * y_pool, w_out[l])
        x = layer_norm(DEEPNORM_ALPHA * x + (1.0 + gate) * y, ln_g[l], ln_b[l])
    return x
```

```python
import functools
import math

import jax
import jax.numpy as jnp
from jax import lax
from jax.experimental import pallas as pl
from jax.experimental.pallas import tpu as pltpu

N_HEADS = 16
QK_NOPE_DIM = 128
QK_ROPE_DIM = 64
QK_HEAD_DIM = QK_NOPE_DIM + QK_ROPE_DIM
V_HEAD_DIM = 128
Q_LORA_RANK = 512
KV_LORA_RANK = 512
ROPE_THETA = 10000.0
POOL_WINDOWS = (2, 4, 8, 16)
POOL_GROUP_DIM = 256
POOL_WIDTH = len(POOL_WINDOWS) * POOL_GROUP_DIM
RMS_EPS = 1e-6
LN_EPS = 1e-5
DEPTH = 1
DEEPNORM_ALPHA = (2.0 * DEPTH) ** 0.25

QK_PAD_DIM = 256
POOL_HALO = 16
VMEM_LIMIT = 56 * 1024 * 1024

F32 = jnp.float32
BF16 = jnp.bfloat16


def _const_spec(shape):
    nd = len(shape)
    return pl.BlockSpec(shape, lambda *_: (0,) * nd, pipeline_mode=pl.Buffered(1))


def _params(sem):
    return pltpu.CompilerParams(dimension_semantics=sem, vmem_limit_bytes=VMEM_LIMIT)


def _mod_kernel(c_ref, w_ref, b_ref, o_ref):
    c = c_ref[...]
    sc = c * jax.nn.sigmoid(c)
    o_ref[...] = jnp.sum(sc * w_ref[...], axis=0, keepdims=True) + b_ref[...]


def _mod(c_col, w_ada, b_ada, tn=512):
    d, n = w_ada.shape
    return pl.pallas_call(
        _mod_kernel,
        grid=(n // tn,),
        in_specs=[
            pl.BlockSpec((d, 1), lambda j: (0, 0)),
            pl.BlockSpec((d, tn), lambda j: (0, j)),
            pl.BlockSpec((1, tn), lambda j: (0, j)),
        ],
        out_specs=pl.BlockSpec((1, tn), lambda j: (0, j)),
        out_shape=jax.ShapeDtypeStruct((1, n), F32),
        compiler_params=_params(("arbitrary",)),
        name="mod",
    )(c_col, w_ada, b_ada)


def _ln_kernel(x_ref, mod_ref, h_ref, *, d):
    x = x_ref[...]
    mu = jnp.mean(x, axis=-1, keepdims=True)
    xc = x - mu
    var = jnp.mean(xc * xc, axis=-1, keepdims=True)
    y = xc * lax.rsqrt(var + LN_EPS)
    shift = mod_ref[:, 0:d]
    scale = mod_ref[:, d:2 * d]
    h_ref[...] = (y * (1.0 + scale) + shift).astype(h_ref.dtype)


def _ln(x2, mod, tm=512):
    s, d = x2.shape
    return pl.pallas_call(
        functools.partial(_ln_kernel, d=d),
        grid=(s // tm,),
        in_specs=[
            pl.BlockSpec((tm, d), lambda i: (i, 0)),
            pl.BlockSpec((1, 3 * d), lambda i: (0, 0)),
        ],
        out_specs=pl.BlockSpec((tm, d), lambda i: (i, 0)),
        out_shape=jax.ShapeDtypeStruct((s, d), BF16),
        compiler_params=_params(("parallel",)),
        name="ln",
    )(x2, mod)


def _rms(xf, g):
    return xf * lax.rsqrt(jnp.mean(xf * xf, axis=-1, keepdims=True) + RMS_EPS) * g


def _lat_kernel(h_ref, wlat_ref, qg_ref, kvg_ref, wq_ref, wkv_ref, pos_ref, freq_ref,
                q_ref, k_ref, v_ref, *, q_scale):
    h = h_ref[...]
    lat = jnp.dot(h, wlat_ref[...], preferred_element_type=F32)
    qn = _rms(lat[:, 0:Q_LORA_RANK], qg_ref[...]).astype(BF16)
    cn = _rms(lat[:, Q_LORA_RANK:Q_LORA_RANK + KV_LORA_RANK], kvg_ref[...]).astype(BF16)
    kpe = lat[:, Q_LORA_RANK + KV_LORA_RANK:]

    ang = pos_ref[...].astype(F32) * freq_ref[...]
    lane = lax.broadcasted_iota(jnp.int32, ang.shape, 1)
    cs = jnp.where(lane < 64, jnp.cos(ang),
                   jnp.where(lane < 96, -jnp.sin(ang), jnp.sin(ang)))
    low = lane < 64

    def rope(xx):
        r = xx * cs
        return r + pltpu.roll(r, 64, 1)

    k_rot = jnp.where(low, rope(kpe), 0.0).astype(BF16)
    for hd in range(N_HEADS):
        qh = jnp.dot(qn, wq_ref[:, hd * QK_PAD_DIM:(hd + 1) * QK_PAD_DIM],
                     preferred_element_type=F32)
        q_ref[hd, :, 0:128] = (qh[:, 0:128] * q_scale).astype(BF16)
        q_ref[hd, :, 128:256] = (rope(qh[:, 128:256]) * q_scale).astype(BF16)
        kvh = jnp.dot(cn, wkv_ref[:, hd * 256:(hd + 1) * 256], preferred_element_type=F32)
        k_ref[hd, :, 0:128] = kvh[:, 0:128].astype(BF16)
        k_ref[hd, :, 128:256] = k_rot
        v_ref[hd] = kvh[:, 128:256].astype(BF16)


def _lat(h, wlat, qg, kvg, wq, wkv, pos_col, freq, tm=256):
    s, d = h.shape
    q_scale = (QK_HEAD_DIM ** -0.5) * math.log2(math.e)
    return pl.pallas_call(
        functools.partial(_lat_kernel, q_scale=q_scale),
        grid=(s // tm,),
        in_specs=[
            pl.BlockSpec((tm, d), lambda i: (i, 0)),
            _const_spec(wlat.shape),
            _const_spec(qg.shape),
            _const_spec(kvg.shape),
            _const_spec(wq.shape),
            _const_spec(wkv.shape),
            pl.BlockSpec((tm, 1), lambda i: (i, 0)),
            _const_spec(freq.shape),
        ],
        out_specs=[
            pl.BlockSpec((N_HEADS, tm, QK_PAD_DIM), lambda i: (0, i, 0)),
            pl.BlockSpec((N_HEADS, tm, QK_PAD_DIM), lambda i: (0, i, 0)),
            pl.BlockSpec((N_HEADS, tm, V_HEAD_DIM), lambda i: (0, i, 0)),
        ],
        out_shape=[
            jax.ShapeDtypeStruct((N_HEADS, s, QK_PAD_DIM), BF16),
            jax.ShapeDtypeStruct((N_HEADS, s, QK_PAD_DIM), BF16),
            jax.ShapeDtypeStruct((N_HEADS, s, V_HEAD_DIM), BF16),
        ],
        compiler_params=_params(("parallel",)),
        name="lat",
    )(h, wlat, qg, kvg, wq, wkv, pos_col, freq)


def _gates_kernel(h_ref, w_ref, b_ref, o_ref, *, tn, d, pool_w):
    j = pl.program_id(1)
    z = jnp.dot(h_ref[...], w_ref[...], preferred_element_type=F32)
    col = j * tn
    is_ident = jnp.logical_and(col >= d, col < d + pool_w)
    is_sigm = col >= d + 2 * pool_w

    @pl.when(is_ident)
    def _():
        o_ref[...] = z.astype(o_ref.dtype)

    @pl.when(is_sigm)
    def _():
        o_ref[...] = jax.nn.sigmoid(z + b_ref[...]).astype(o_ref.dtype)

    @pl.when(jnp.logical_not(jnp.logical_or(is_ident, is_sigm)))
    def _():
        o_ref[...] = (z * jax.nn.sigmoid(z)).astype(o_ref.dtype)


def _gates(h, wg, bg, tm=1024, tn=1024):
    s, d = h.shape
    n = wg.shape[1]
    return pl.pallas_call(
        functools.partial(_gates_kernel, tn=tn, d=d, pool_w=POOL_WIDTH),
        grid=(s // tm, n // tn),
        in_specs=[
            pl.BlockSpec((tm, d), lambda i, j: (i, 0)),
            pl.BlockSpec((d, tn), lambda i, j: (0, j)),
            pl.BlockSpec((1, tn), lambda i, j: (0, j)),
        ],
        out_specs=pl.BlockSpec((tm, tn), lambda i, j: (i, j)),
        out_shape=jax.ShapeDtypeStruct((s, n), BF16),
        compiler_params=_params(("parallel", "arbitrary")),
        name="gates",
    )(h, wg, bg)


def _attn_kernel(q_ref, k_ref, v_ref, sg_ref, o_ref, m_sc, l_sc, acc_sc, *, blk):
    qi = pl.program_id(1)
    q = q_ref[0]
    m_sc[...] = jnp.full(m_sc.shape, -jnp.inf, F32)
    l_sc[...] = jnp.zeros(l_sc.shape, F32)
    acc_sc[...] = jnp.zeros(acc_sc.shape, F32)

    def step(j, masked):
        start = pl.multiple_of(j * blk, blk)
        k = k_ref[0, pl.ds(start, blk), :]
        v = v_ref[0, pl.ds(start, blk), :]
        s = lax.dot_general(q, k, (((1,), (1,)), ((), ())), preferred_element_type=F32)
        if masked:
            row = lax.broadcasted_iota(jnp.int32, s.shape, 0)
            colk = lax.broadcasted_iota(jnp.int32, s.shape, 1)
            s = jnp.where(colk <= row, s, -jnp.inf)
        m_prev = m_sc[...]
        m_new = jnp.maximum(m_prev, jnp.max(s, axis=1, keepdims=True))
        p = jnp.exp2(s - m_new)
        alpha = jnp.exp2(m_prev - m_new)
        l_sc[...] = alpha * l_sc[...] + jnp.sum(p, axis=1, keepdims=True)
        acc_sc[...] = alpha * acc_sc[...] + jnp.dot(p.astype(BF16), v,
                                                    preferred_element_type=F32)
        m_sc[...] = m_new

    def body(j, carry):
        step(j, False)
        return carry

    lax.fori_loop(0, qi, body, 0)
    step(qi, True)
    o = acc_sc[...] * (1.0 / l_sc[...])
    o_ref[...] = (o * sg_ref[...].astype(F32)).astype(o_ref.dtype)


def _attn(q, k, v, g, blk=512):
    nh, s, _ = q.shape
    return pl.pallas_call(
        functools.partial(_attn_kernel, blk=blk),
        grid=(nh, s // blk),
        in_specs=[
            pl.BlockSpec((1, blk, QK_PAD_DIM), lambda h, i: (h, i, 0)),
            pl.BlockSpec((1, s, QK_PAD_DIM), lambda h, i: (h, 0, 0)),
            pl.BlockSpec((1, s, V_HEAD_DIM), lambda h, i: (h, 0, 0)),
            pl.BlockSpec((blk, V_HEAD_DIM), lambda h, i: (i, h)),
        ],
        out_specs=pl.BlockSpec((blk, V_HEAD_DIM), lambda h, i: (i, h)),
        out_shape=jax.ShapeDtypeStruct((s, nh * V_HEAD_DIM), BF16),
        scratch_shapes=[
            pltpu.VMEM((blk, 1), F32),
            pltpu.VMEM((blk, 1), F32),
            pltpu.VMEM((blk, V_HEAD_DIM), F32),
        ],
        compiler_params=_params(("arbitrary", "arbitrary")),
        name="attn",
    )(q, k, v, g)


def _pool_kernel(u_ref, halo_ref, pg_ref, w_ref, sc_ref, o_ref, *, tm):
    i = pl.program_id(0)
    u = u_ref[...].astype(F32)
    halo = jnp.where(i > 0, halo_ref[...].astype(F32), 0.0)
    ext = jnp.concatenate([halo, u], axis=0)
    t = i * tm + lax.broadcasted_iota(jnp.int32, (tm, 1), 0)
    n_ext = POOL_HALO + tm

    def shifted(a, kk):
        return jnp.concatenate([jnp.zeros((kk, a.shape[1]), F32), a[:a.shape[0] - kk]], axis=0)

    acc = ext
    width = 1
    for g, w in enumerate(POOL_WINDOWS):
        lo = g * POOL_GROUP_DIM
        acc = acc[:, (0 if g == 0 else POOL_GROUP_DIM):]
        while width < w:
            acc = acc + shifted(acc, width)
            width *= 2
        win = acc[POOL_HALO:n_ext, 0:POOL_GROUP_DIM]
        cnt = jnp.minimum(t + 1, w).astype(F32)
        pooled = win / cnt - u[:, lo:lo + POOL_GROUP_DIM]
        mixed = jnp.dot(pooled.astype(BF16), w_ref[g], preferred_element_type=F32)
        mixed = mixed * sc_ref[:, lo:lo + POOL_GROUP_DIM]
        o_ref[:, lo:lo + POOL_GROUP_DIM] = (
            mixed * pg_ref[:, lo:lo + POOL_GROUP_DIM].astype(F32)).astype(o_ref.dtype)


def _pool(g, w_pool, pool_scale, d, tm=512):
    s = g.shape[0]
    p = POOL_WIDTH
    ub = d // p
    hb = tm // POOL_HALO
    return pl.pallas_call(
        functools.partial(_pool_kernel, tm=tm),
        grid=(s // tm,),
        in_specs=[
            pl.BlockSpec((tm, p), lambda i: (i, ub)),
            pl.BlockSpec((POOL_HALO, p), lambda i: (jnp.maximum(i * hb - 1, 0), ub)),
            pl.BlockSpec((tm, p), lambda i: (i, ub + 1)),
            _const_spec(w_pool.shape),
            _const_spec(pool_scale.shape),
        ],
        out_specs=pl.BlockSpec((tm, p), lambda i: (i, 0)),
        out_shape=jax.ShapeDtypeStruct((s, p), BF16),
        compiler_params=_params(("parallel",)),
        name="pool",
    )(g, g, g, w_pool, pool_scale)


def _final_kernel(ya_ref, yp_ref, gm_ref, gp_ref, x_ref, mod_ref, lng_ref, lnb_ref,
                  wmo_ref, wpo_ref, wout_ref, o_ref, *, d):
    y_mla = jnp.dot(ya_ref[...], wmo_ref[...], preferred_element_type=F32)
    y_pool = jnp.dot(yp_ref[...], wpo_ref[...], preferred_element_type=F32)
    merged = gm_ref[...].astype(F32) * y_mla + gp_ref[...].astype(F32) * y_pool
    y = jnp.dot(merged.astype(BF16), wout_ref[...], preferred_element_type=F32)
    gate = mod_ref[:, 2 * d:3 * d]
    r = DEEPNORM_ALPHA * x_ref[...] + (1.0 + gate) * y
    mu = jnp.mean(r, axis=-1, keepdims=True)
    rc = r - mu
    var = jnp.mean(rc * rc, axis=-1, keepdims=True)
    o_ref[...] = rc * lax.rsqrt(var + LN_EPS) * lng_ref[...] + lnb_ref[...]


def _final(ya, yp, g, x2, mod, ln_g, ln_b, wmo, wpo, wout, tm=512):
    s, d = x2.shape
    gb = (d + 2 * POOL_WIDTH) // d
    return pl.pallas_call(
        functools.partial(_final_kernel, d=d),
        grid=(s // tm,),
        in_specs=[
            pl.BlockSpec((tm, ya.shape[1]), lambda i: (i, 0)),
            pl.BlockSpec((tm, yp.shape[1]), lambda i: (i, 0)),
            pl.BlockSpec((tm, d), lambda i: (i, gb)),
            pl.BlockSpec((tm, d), lambda i: (i, gb + 1)),
            pl.BlockSpec((tm, d), lambda i: (i, 0)),
            _const_spec(mod.shape),
            _const_spec(ln_g.shape),
            _const_spec(ln_b.shape),
            _const_spec(wmo.shape),
            _const_spec(wpo.shape),
            _const_spec(wout.shape),
        ],
        out_specs=pl.BlockSpec((tm, d), lambda i: (i, 0)),
        out_shape=jax.ShapeDtypeStruct((s, d), F32),
        compiler_params=_params(("parallel",)),
        name="final",
    )(ya, yp, g, g, x2, mod, ln_g, ln_b, wmo, wpo, wout)


def _swap_halves(w):
    half = QK_ROPE_DIM // 2
    return jnp.concatenate([w[..., half:], w[..., :half]], axis=-1)


def kernel(x, c, positions, w_ada, b_ada, w_in, b_gates, q_norm_g, w_q_b, kv_norm_g, w_kv_b,
           w_mla_o, w_pool_g, pool_scale, w_pool_o, w_out, ln_g, ln_b):
    b, s, d = x.shape
    assert b == 1 and w_ada.shape[0] == 1
    l = 0
    x2 = x.reshape(s, d)

    o_q, o_kv = Q_LORA_RANK, Q_LORA_RANK + KV_LORA_RANK
    o_g = o_kv + QK_ROPE_DIM
    w_kpe = w_in[l][:, o_kv:o_g]
    wlat = jnp.concatenate([w_in[l][:, :o_kv], w_kpe, _swap_halves(w_kpe)], axis=1).astype(BF16)
    wg = w_in[l][:, o_g:].astype(BF16)
    bg = jnp.concatenate([jnp.zeros((d + 2 * POOL_WIDTH,), F32), b_gates[l]]).reshape(1, -1)
    wq3 = w_q_b[l].reshape(Q_LORA_RANK, N_HEADS, QK_HEAD_DIM)
    wq_pe = wq3[..., QK_NOPE_DIM:]
    wq = jnp.concatenate([wq3[..., :QK_NOPE_DIM], wq_pe, _swap_halves(wq_pe)], axis=-1)
    wq = wq.reshape(Q_LORA_RANK, N_HEADS * QK_PAD_DIM).astype(BF16)
    wkv = w_kv_b[l].astype(BF16)
    inv_freq = ROPE_THETA ** (-jnp.arange(0, QK_ROPE_DIM, 2, dtype=F32) / QK_ROPE_DIM)
    freq = jnp.tile(inv_freq, 4).reshape(1, 128)
    pos_col = positions.reshape(s, 1)

    mod = _mod(c.reshape(d, 1), w_ada[l], b_ada[l].reshape(1, -1))
    h = _ln(x2, mod)
    q, k, v = _lat(h, wlat, q_norm_g[l].reshape(1, -1), kv_norm_g[l].reshape(1, -1),
                   wq, wkv, pos_col, freq)
    g = _gates(h, wg, bg)
    ya = _attn(q, k, v, g)
    yp = _pool(g, w_pool_g[l].astype(BF16), pool_scale[l].reshape(1, -1), d)
    out = _final(ya, yp, g, x2, mod, ln_g[l].reshape(1, -1), ln_b[l].reshape(1, -1),
                 w_mla_o[l].astype(BF16), w_pool_o[l].astype(BF16), w_out[l].astype(BF16))
    return out.reshape(b, s, d)
```

```python
import functools
import math

import jax
import jax.numpy as jnp
from jax import lax
from jax.experimental import pallas as pl
from jax.experimental.pallas import tpu as pltpu

N_HEADS = 16
QK_NOPE_DIM = 128
QK_ROPE_DIM = 64
QK_HEAD_DIM = QK_NOPE_DIM + QK_ROPE_DIM
V_HEAD_DIM = 128
Q_LORA_RANK = 512
KV_LORA_RANK = 512
ROPE_THETA = 10000.0
POOL_WINDOWS = (2, 4, 8, 16)
POOL_GROUP_DIM = 256
POOL_WIDTH = len(POOL_WINDOWS) * POOL_GROUP_DIM
RMS_EPS = 1e-6
LN_EPS = 1e-5
DEPTH = 1
DEEPNORM_ALPHA = (2.0 * DEPTH) ** 0.25

QK_PAD_DIM = 256
POOL_HALO = 16
VMEM_LIMIT = 56 * 1024 * 1024

F32 = jnp.float32
BF16 = jnp.bfloat16


def _const_spec(shape):
    nd = len(shape)
    return pl.BlockSpec(shape, lambda *_: (0,) * nd, pipeline_mode=pl.Buffered(1))


def _params(sem):
    return pltpu.CompilerParams(dimension_semantics=sem, vmem_limit_bytes=VMEM_LIMIT)


def _mod_kernel(c_ref, w_ref, b_ref, o_ref):
    c = c_ref[...]
    sc = c * jax.nn.sigmoid(c)
    o_ref[...] = jnp.sum(sc * w_ref[...], axis=0, keepdims=True) + b_ref[...]


def _mod(c_col, w_ada, b_ada, tn=512):
    d, n = w_ada.shape
    return pl.pallas_call(
        _mod_kernel,
        grid=(n // tn,),
        in_specs=[
            pl.BlockSpec((d, 1), lambda j: (0, 0)),
            pl.BlockSpec((d, tn), lambda j: (0, j)),
            pl.BlockSpec((1, tn), lambda j: (0, j)),
        ],
        out_specs=pl.BlockSpec((1, tn), lambda j: (0, j)),
        out_shape=jax.ShapeDtypeStruct((1, n), F32),
        compiler_params=_params(("arbitrary",)),
        name="mod",
    )(c_col, w_ada, b_ada)


def _ln_kernel(x_ref, mod_ref, h_ref, *, d):
    x = x_ref[...]
    mu = jnp.mean(x, axis=-1, keepdims=True)
    xc = x - mu
    var = jnp.mean(xc * xc, axis=-1, keepdims=True)
    y = xc * lax.rsqrt(var + LN_EPS)
    shift = mod_ref[:, 0:d]
    scale = mod_ref[:, d:2 * d]
    h_ref[...] = (y * (1.0 + scale) + shift).astype(h_ref.dtype)


def _ln(x2, mod, tm=512):
    s, d = x2.shape
    return pl.pallas_call(
        functools.partial(_ln_kernel, d=d),
        grid=(s // tm,),
        in_specs=[
            pl.BlockSpec((tm, d), lambda i: (i, 0)),
            pl.BlockSpec((1, 3 * d), lambda i: (0, 0)),
        ],
        out_specs=pl.BlockSpec((tm, d), lambda i: (i, 0)),
        out_shape=jax.ShapeDtypeStruct((s, d), BF16),
        compiler_params=_params(("parallel",)),
        name="ln",
    )(x2, mod)


def _rms(xf, g):
    return xf * lax.rsqrt(jnp.mean(xf * xf, axis=-1, keepdims=True) + RMS_EPS) * g


def _lat_kernel(h_ref, wlat_ref, qg_ref, kvg_ref, wq_ref, wkv_ref, pos_ref, freq_ref,
                q_ref, k_ref, v_ref, *, q_scale):
    h = h_ref[...]
    lat = jnp.dot(h, wlat_ref[...], preferred_element_type=F32)
    qn = _rms(lat[:, 0:Q_LORA_RANK], qg_ref[...]).astype(BF16)
    cn = _rms(lat[:, Q_LORA_RANK:Q_LORA_RANK + KV_LORA_RANK], kvg_ref[...]).astype(BF16)
    kpe = lat[:, Q_LORA_RANK + KV_LORA_RANK:]

    ang = pos_ref[...].astype(F32) * freq_ref[...]
    lane = lax.broadcasted_iota(jnp.int32, ang.shape, 1)
    cs = jnp.where(lane < 64, jnp.cos(ang),
                   jnp.where(lane < 96, -jnp.sin(ang), jnp.sin(ang)))
    low = lane < 64

    def rope(xx):
        r = xx * cs
        return r + pltpu.roll(r, 64, 1)

    k_rot = jnp.where(low, rope(kpe), 0.0).astype(BF16)
    for hd in range(N_HEADS):
        qh = jnp.dot(qn, wq_ref[:, hd * QK_PAD_DIM:(hd + 1) * QK_PAD_DIM],
                     preferred_element_type=F32)
        q_ref[hd, :, 0:128] = (qh[:, 0:128] * q_scale).astype(BF16)
        q_ref[hd, :, 128:256] = (rope(qh[:, 128:256]) * q_scale).astype(BF16)
        kvh = jnp.dot(cn, wkv_ref[:, hd * 256:(hd + 1) * 256], preferred_element_type=F32)
        k_ref[hd, :, 0:128] = kvh[:, 0:128].astype(BF16)
        k_ref[hd, :, 128:256] = k_rot
        v_ref[hd, :, 0:128] = kvh[:, 128:256].astype(BF16)
        v_ref[hd, :, 128:256] = jnp.ones((kvh.shape[0], V_HEAD_DIM), BF16)


def _lat(h, wlat, qg, kvg, wq, wkv, pos_col, freq, tm=256):
    s, d = h.shape
    q_scale = (QK_HEAD_DIM ** -0.5) * math.log2(math.e)
    return pl.pallas_call(
        functools.partial(_lat_kernel, q_scale=q_scale),
        grid=(s // tm,),
        in_specs=[
            pl.BlockSpec((tm, d), lambda i: (i, 0)),
            _const_spec(wlat.shape),
            _const_spec(qg.shape),
            _const_spec(kvg.shape),
            _const_spec(wq.shape),
            _const_spec(wkv.shape),
            pl.BlockSpec((tm, 1), lambda i: (i, 0)),
            _const_spec(freq.shape),
        ],
        out_specs=[
            pl.BlockSpec((N_HEADS, tm, QK_PAD_DIM), lambda i: (0, i, 0)),
            pl.BlockSpec((N_HEADS, tm, QK_PAD_DIM), lambda i: (0, i, 0)),
            pl.BlockSpec((N_HEADS, tm, 2 * V_HEAD_DIM), lambda i: (0, i, 0)),
        ],
        out_shape=[
            jax.ShapeDtypeStruct((N_HEADS, s, QK_PAD_DIM), BF16),
            jax.ShapeDtypeStruct((N_HEADS, s, QK_PAD_DIM), BF16),
            jax.ShapeDtypeStruct((N_HEADS, s, 2 * V_HEAD_DIM), BF16),
        ],
        compiler_params=_params(("parallel",)),
        name="lat",
    )(h, wlat, qg, kvg, wq, wkv, pos_col, freq)


def _gates_kernel(h_ref, w_ref, b_ref, o_ref, *, tn, d, pool_w):
    j = pl.program_id(1)
    z = jnp.dot(h_ref[...], w_ref[...], preferred_element_type=F32)
    col = j * tn
    is_ident = jnp.logical_and(col >= d, col < d + pool_w)
    is_sigm = col >= d + 2 * pool_w

    @pl.when(is_ident)
    def _():
        o_ref[...] = z.astype(o_ref.dtype)

    @pl.when(is_sigm)
    def _():
        o_ref[...] = jax.nn.sigmoid(z + b_ref[...]).astype(o_ref.dtype)

    @pl.when(jnp.logical_not(jnp.logical_or(is_ident, is_sigm)))
    def _():
        o_ref[...] = (z * jax.nn.sigmoid(z)).astype(o_ref.dtype)


def _gates(h, wg, bg, tm=1024, tn=1024):
    s, d = h.shape
    n = wg.shape[1]
    return pl.pallas_call(
        functools.partial(_gates_kernel, tn=tn, d=d, pool_w=POOL_WIDTH),
        grid=(s // tm, n // tn),
        in_specs=[
            pl.BlockSpec((tm, d), lambda i, j: (i, 0)),
            pl.BlockSpec((d, tn), lambda i, j: (0, j)),
            pl.BlockSpec((1, tn), lambda i, j: (0, j)),
        ],
        out_specs=pl.BlockSpec((tm, tn), lambda i, j: (i, j)),
        out_shape=jax.ShapeDtypeStruct((s, n), BF16),
        compiler_params=_params(("parallel", "arbitrary")),
        name="gates",
    )(h, wg, bg)


_NT_DIMS = (((1,), (1,)), ((), ()))


def _attn_kernel(q_ref, k_ref, v_ref, sg_ref, o_ref, sa_sc, sb_sc, m_sc, acc_sc, *, blk, rc):
    qi = pl.program_id(1)
    q = q_ref[0]
    m_sc[...] = jnp.full(m_sc.shape, -jnp.inf, F32)
    acc_sc[...] = jnp.zeros(acc_sc.shape, F32)

    def qk(j, s_sc):
        start = pl.multiple_of(j * blk, blk)
        s_sc[...] = lax.dot_general(q, k_ref[0, pl.ds(start, blk), :], _NT_DIMS,
                                    preferred_element_type=F32)

    def softmax_pv(j, s_sc, masked):
        start = pl.multiple_of(j * blk, blk)
        for r in range(blk // rc):
            rows = slice(r * rc, (r + 1) * rc)
            ncol = (r + 1) * rc if masked else blk
            s = s_sc[rows, 0:ncol]
            if masked:
                row = r * rc + lax.broadcasted_iota(jnp.int32, s.shape, 0)
                colk = lax.broadcasted_iota(jnp.int32, s.shape, 1)
                s = jnp.where(colk <= row, s, -jnp.inf)
            m_prev = m_sc[rows, :]
            m_new = jnp.maximum(m_prev, jnp.max(s, axis=1, keepdims=True))
            alpha = jnp.exp2(m_prev - m_new)
            p = jnp.concatenate(
                [jnp.exp2(s[:, c * 128:(c + 1) * 128] - m_new) for c in range(ncol // 128)],
                axis=1).astype(BF16)
            v = v_ref[0, pl.ds(start, ncol), :]
            pv = jnp.dot(p, v, preferred_element_type=F32)
            acc_sc[rows, :] = jnp.concatenate([alpha, alpha], axis=1) * acc_sc[rows, :] + pv
            m_sc[rows, :] = m_new

    qk(0, sa_sc)

    def pair(t, carry):
        a = 2 * t
        qk(a + 1, sb_sc)
        softmax_pv(a, sa_sc, False)
        qk(a + 2, sa_sc)
        softmax_pv(a + 1, sb_sc, False)
        return carry

    lax.fori_loop(0, lax.shift_right_logical(qi, 1), pair, 0)
    odd = lax.bitwise_and(qi, 1) == 1

    @pl.when(jnp.logical_not(odd))
    def _():
        softmax_pv(qi, sa_sc, True)

    @pl.when(odd)
    def _():
        qk(qi, sb_sc)
        softmax_pv(qi - 1, sa_sc, False)
        softmax_pv(qi, sb_sc, True)

    acc = acc_sc[...]
    o = acc[:, 0:V_HEAD_DIM] / acc[:, V_HEAD_DIM:2 * V_HEAD_DIM]
    o_ref[...] = (o * sg_ref[...].astype(F32)).astype(o_ref.dtype)


def _attn(q, k, v1, g, blk=512, rc=256):
    nh, s, _ = q.shape
    return pl.pallas_call(
        functools.partial(_attn_kernel, blk=blk, rc=rc),
        grid=(nh, s // blk),
        in_specs=[
            pl.BlockSpec((1, blk, QK_PAD_DIM), lambda h, i: (h, i, 0)),
            pl.BlockSpec((1, s, QK_PAD_DIM), lambda h, i: (h, 0, 0)),
            pl.BlockSpec((1, s, 2 * V_HEAD_DIM), lambda h, i: (h, 0, 0)),
            pl.BlockSpec((blk, V_HEAD_DIM), lambda h, i: (i, h)),
        ],
        out_specs=pl.BlockSpec((blk, V_HEAD_DIM), lambda h, i: (i, h)),
        out_shape=jax.ShapeDtypeStruct((s, nh * V_HEAD_DIM), BF16),
        scratch_shapes=[
            pltpu.VMEM((blk, blk), F32),
            pltpu.VMEM((blk, blk), F32),
            pltpu.VMEM((blk, 128), F32),
            pltpu.VMEM((blk, 2 * V_HEAD_DIM), F32),
        ],
        compiler_params=_params(("arbitrary", "arbitrary")),
        name="attn",
    )(q, k, v1, g)


def _pool_kernel(u_ref, halo_ref, pg_ref, w_ref, sc_ref, o_ref, *, tm):
    i = pl.program_id(0)
    u = u_ref[...].astype(F32)
    halo = jnp.where(i > 0, halo_ref[...].astype(F32), 0.0)
    ext = jnp.concatenate([halo, u], axis=0)
    t = i * tm + lax.broadcasted_iota(jnp.int32, (tm, 1), 0)
    n_ext = POOL_HALO + tm

    def shifted(a, kk):
        return jnp.concatenate([jnp.zeros((kk, a.shape[1]), F32), a[:a.shape[0] - kk]], axis=0)

    acc = ext
    width = 1
    for g, w in enumerate(POOL_WINDOWS):
        lo = g * POOL_GROUP_DIM
        acc = acc[:, (0 if g == 0 else POOL_GROUP_DIM):]
        while width < w:
            acc = acc + shifted(acc, width)
            width *= 2
        win = acc[POOL_HALO:n_ext, 0:POOL_GROUP_DIM]
        cnt = jnp.minimum(t + 1, w).astype(F32)
        pooled = win / cnt - u[:, lo:lo + POOL_GROUP_DIM]
        mixed = jnp.dot(pooled.astype(BF16), w_ref[g], preferred_element_type=F32)
        mixed = mixed * sc_ref[:, lo:lo + POOL_GROUP_DIM]
        o_ref[:, lo:lo + POOL_GROUP_DIM] = (
            mixed * pg_ref[:, lo:lo + POOL_GROUP_DIM].astype(F32)).astype(o_ref.dtype)


def _pool(g, w_pool, pool_scale, d, tm=512):
    s = g.shape[0]
    p = POOL_WIDTH
    ub = d // p
    hb = tm // POOL_HALO
    return pl.pallas_call(
        functools.partial(_pool_kernel, tm=tm),
        grid=(s // tm,),
        in_specs=[
            pl.BlockSpec((tm, p), lambda i: (i, ub)),
            pl.BlockSpec((POOL_HALO, p), lambda i: (jnp.maximum(i * hb - 1, 0), ub)),
            pl.BlockSpec((tm, p), lambda i: (i, ub + 1)),
            _const_spec(w_pool.shape),
            _const_spec(pool_scale.shape),
        ],
        out_specs=pl.BlockSpec((tm, p), lambda i: (i, 0)),
        out_shape=jax.ShapeDtypeStruct((s, p), BF16),
        compiler_params=_params(("parallel",)),
        name="pool",
    )(g, g, g, w_pool, pool_scale)


def _final_kernel(ya_ref, yp_ref, gm_ref, gp_ref, x_ref, mod_ref, lng_ref, lnb_ref,
                  wmo_ref, wpo_ref, wout_ref, o_ref, *, d):
    y_mla = jnp.dot(ya_ref[...], wmo_ref[...], preferred_element_type=F32)
    y_pool = jnp.dot(yp_ref[...], wpo_ref[...], preferred_element_type=F32)
    merged = gm_ref[...].astype(F32) * y_mla + gp_ref[...].astype(F32) * y_pool
    y = jnp.dot(merged.astype(BF16), wout_ref[...], preferred_element_type=F32)
    gate = mod_ref[:, 2 * d:3 * d]
    r = DEEPNORM_ALPHA * x_ref[...] + (1.0 + gate) * y
    mu = jnp.mean(r, axis=-1, keepdims=True)
    rc = r - mu
    var = jnp.mean(rc * rc, axis=-1, keepdims=True)
    o_ref[...] = rc * lax.rsqrt(var + LN_EPS) * lng_ref[...] + lnb_ref[...]


def _final(ya, yp, g, x2, mod, ln_g, ln_b, wmo, wpo, wout, tm=512):
    s, d = x2.shape
    gb = (d + 2 * POOL_WIDTH) // d
    return pl.pallas_call(
        functools.partial(_final_kernel, d=d),
        grid=(s // tm,),
        in_specs=[
            pl.BlockSpec((tm, ya.shape[1]), lambda i: (i, 0)),
            pl.BlockSpec((tm, yp.shape[1]), lambda i: (i, 0)),
            pl.BlockSpec((tm, d), lambda i: (i, gb)),
            pl.BlockSpec((tm, d), lambda i: (i, gb + 1)),
            pl.BlockSpec((tm, d), lambda i: (i, 0)),
            _const_spec(mod.shape),
            _const_spec(ln_g.shape),
            _const_spec(ln_b.shape),
            _const_spec(wmo.shape),
            _const_spec(wpo.shape),
            _const_spec(wout.shape),
        ],
        out_specs=pl.BlockSpec((tm, d), lambda i: (i, 0)),
        out_shape=jax.ShapeDtypeStruct((s, d), F32),
        compiler_params=_params(("parallel",)),
        name="final",
    )(ya, yp, g, g, x2, mod, ln_g, ln_b, wmo, wpo, wout)


def _swap_halves(w):
    half = QK_ROPE_DIM // 2
    return jnp.concatenate([w[..., half:], w[..., :half]], axis=-1)


def kernel(x, c, positions, w_ada, b_ada, w_in, b_gates, q_norm_g, w_q_b, kv_norm_g, w_kv_b,
           w_mla_o, w_pool_g, pool_scale, w_pool_o, w_out, ln_g, ln_b):
    b, s, d = x.shape
    assert b == 1 and w_ada.shape[0] == 1
    l = 0
    x2 = x.reshape(s, d)

    o_q, o_kv = Q_LORA_RANK, Q_LORA_RANK + KV_LORA_RANK
    o_g = o_kv + QK_ROPE_DIM
    w_kpe = w_in[l][:, o_kv:o_g]
    wlat = jnp.concatenate([w_in[l][:, :o_kv], w_kpe, _swap_halves(w_kpe)], axis=1).astype(BF16)
    wg = w_in[l][:, o_g:].astype(BF16)
    bg = jnp.concatenate([jnp.zeros((d + 2 * POOL_WIDTH,), F32), b_gates[l]]).reshape(1, -1)
    wq3 = w_q_b[l].reshape(Q_LORA_RANK, N_HEADS, QK_HEAD_DIM)
    wq_pe = wq3[..., QK_NOPE_DIM:]
    wq = jnp.concatenate([wq3[..., :QK_NOPE_DIM], wq_pe, _swap_halves(wq_pe)], axis=-1)
    wq = wq.reshape(Q_LORA_RANK, N_HEADS * QK_PAD_DIM).astype(BF16)
    wkv = w_kv_b[l].astype(BF16)
    inv_freq = ROPE_THETA ** (-jnp.arange(0, QK_ROPE_DIM, 2, dtype=F32) / QK_ROPE_DIM)
    freq = jnp.tile(inv_freq, 4).reshape(1, 128)
    pos_col = positions.reshape(s, 1)

    mod = _mod(c.reshape(d, 1), w_ada[l], b_ada[l].reshape(1, -1))
    h = _ln(x2, mod)
    q, k, v = _lat(h, wlat, q_norm_g[l].reshape(1, -1), kv_norm_g[l].reshape(1, -1),
                   wq, wkv, pos_col, freq)
    g = _gates(h, wg, bg)
    ya = _attn(q, k, v, g)
    yp = _pool(g, w_pool_g[l].astype(BF16), pool_scale[l].reshape(1, -1), d)
    out = _final(ya, yp, g, x2, mod, ln_g[l].reshape(1, -1), ln_b[l].reshape(1, -1),
                 w_mla_o[l].astype(BF16), w_pool_o[l].astype(BF16), w_out[l].astype(BF16))
    return out.reshape(b, s, d)
```

```python
import functools
import math

import jax
import jax.numpy as jnp
from jax import lax
from jax.experimental import pallas as pl
from jax.experimental.pallas import tpu as pltpu

N_HEADS = 16
QK_NOPE_DIM = 128
QK_ROPE_DIM = 64
QK_HEAD_DIM = QK_NOPE_DIM + QK_ROPE_DIM
V_HEAD_DIM = 128
Q_LORA_RANK = 512
KV_LORA_RANK = 512
ROPE_THETA = 10000.0
POOL_WINDOWS = (2, 4, 8, 16)
POOL_GROUP_DIM = 256
POOL_WIDTH = len(POOL_WINDOWS) * POOL_GROUP_DIM
RMS_EPS = 1e-6
LN_EPS = 1e-5
DEPTH = 1
DEEPNORM_ALPHA = (2.0 * DEPTH) ** 0.25

QK_PAD_DIM = 256
POOL_HALO = 16
VMEM_LIMIT = 56 * 1024 * 1024

F32 = jnp.float32
BF16 = jnp.bfloat16


def _const_spec(shape):
    nd = len(shape)
    return pl.BlockSpec(shape, lambda *_: (0,) * nd, pipeline_mode=pl.Buffered(1))


def _params(sem):
    return pltpu.CompilerParams(dimension_semantics=sem, vmem_limit_bytes=VMEM_LIMIT)


def _mod_kernel(c_ref, w_ref, b_ref, o_ref):
    c = c_ref[...]
    sc = c * jax.nn.sigmoid(c)
    o_ref[...] = jnp.sum(sc * w_ref[...], axis=0, keepdims=True) + b_ref[...]


def _mod(c_col, w_ada, b_ada, tn=512):
    d, n = w_ada.shape
    return pl.pallas_call(
        _mod_kernel,
        grid=(n // tn,),
        in_specs=[
            pl.BlockSpec((d, 1), lambda j: (0, 0)),
            pl.BlockSpec((d, tn), lambda j: (0, j)),
            pl.BlockSpec((1, tn), lambda j: (0, j)),
        ],
        out_specs=pl.BlockSpec((1, tn), lambda j: (0, j)),
        out_shape=jax.ShapeDtypeStruct((1, n), F32),
        compiler_params=_params(("arbitrary",)),
        name="mod",
    )(c_col, w_ada, b_ada)


def _ln_kernel(x_ref, mod_ref, h_ref, *, d):
    x = x_ref[...]
    mu = jnp.mean(x, axis=-1, keepdims=True)
    xc = x - mu
    var = jnp.mean(xc * xc, axis=-1, keepdims=True)
    y = xc * lax.rsqrt(var + LN_EPS)
    shift = mod_ref[:, 0:d]
    scale = mod_ref[:, d:2 * d]
    h_ref[...] = (y * (1.0 + scale) + shift).astype(h_ref.dtype)


def _ln(x2, mod, tm=512):
    s, d = x2.shape
    return pl.pallas_call(
        functools.partial(_ln_kernel, d=d),
        grid=(s // tm,),
        in_specs=[
            pl.BlockSpec((tm, d), lambda i: (i, 0)),
            pl.BlockSpec((1, 3 * d), lambda i: (0, 0)),
        ],
        out_specs=pl.BlockSpec((tm, d), lambda i: (i, 0)),
        out_shape=jax.ShapeDtypeStruct((s, d), BF16),
        compiler_params=_params(("parallel",)),
        name="ln",
    )(x2, mod)


def _rms(xf, g):
    return xf * lax.rsqrt(jnp.mean(xf * xf, axis=-1, keepdims=True) + RMS_EPS) * g


def _lat_kernel(h_ref, wlat_ref, qg_ref, kvg_ref, wq_ref, wkv_ref, pos_ref, freq_ref,
                q_ref, k_ref, v_ref, *, q_scale):
    h = h_ref[...]
    lat = jnp.dot(h, wlat_ref[...], preferred_element_type=F32)
    qn = _rms(lat[:, 0:Q_LORA_RANK], qg_ref[...]).astype(BF16)
    cn = _rms(lat[:, Q_LORA_RANK:Q_LORA_RANK + KV_LORA_RANK], kvg_ref[...]).astype(BF16)
    kpe = lat[:, Q_LORA_RANK + KV_LORA_RANK:]

    ang = pos_ref[...].astype(F32) * freq_ref[...]
    lane = lax.broadcasted_iota(jnp.int32, ang.shape, 1)
    cs = jnp.where(lane < 64, jnp.cos(ang),
                   jnp.where(lane < 96, -jnp.sin(ang), jnp.sin(ang)))
    low = lane < 64

    def rope(xx):
        r = xx * cs
        return r + pltpu.roll(r, 64, 1)

    k_rot = jnp.where(low, rope(kpe), 0.0).astype(BF16)
    for hd in range(N_HEADS):
        qh = jnp.dot(qn, wq_ref[:, hd * QK_PAD_DIM:(hd + 1) * QK_PAD_DIM],
                     preferred_element_type=F32)
        q_ref[hd, :, 0:128] = (qh[:, 0:128] * q_scale).astype(BF16)
        q_ref[hd, :, 128:256] = (rope(qh[:, 128:256]) * q_scale).astype(BF16)
        kvh = jnp.dot(cn, wkv_ref[:, hd * 256:(hd + 1) * 256], preferred_element_type=F32)
        k_ref[hd, :, 0:128] = kvh[:, 0:128].astype(BF16)
        k_ref[hd, :, 128:256] = k_rot
        v_ref[hd, :, 0:128] = kvh[:, 128:256].astype(BF16)
        v_ref[hd, :, 128:256] = jnp.ones((kvh.shape[0], V_HEAD_DIM), BF16)


def _lat(h, wlat, qg, kvg, wq, wkv, pos_col, freq, tm=256):
    s, d = h.shape
    q_scale = (QK_HEAD_DIM ** -0.5) * math.log2(math.e)
    return pl.pallas_call(
        functools.partial(_lat_kernel, q_scale=q_scale),
        grid=(s // tm,),
        in_specs=[
            pl.BlockSpec((tm, d), lambda i: (i, 0)),
            _const_spec(wlat.shape),
            _const_spec(qg.shape),
            _const_spec(kvg.shape),
            _const_spec(wq.shape),
            _const_spec(wkv.shape),
            pl.BlockSpec((tm, 1), lambda i: (i, 0)),
            _const_spec(freq.shape),
        ],
        out_specs=[
            pl.BlockSpec((N_HEADS, tm, QK_PAD_DIM), lambda i: (0, i, 0)),
            pl.BlockSpec((N_HEADS, tm, QK_PAD_DIM), lambda i: (0, i, 0)),
            pl.BlockSpec((N_HEADS, tm, 2 * V_HEAD_DIM), lambda i: (0, i, 0)),
        ],
        out_shape=[
            jax.ShapeDtypeStruct((N_HEADS, s, QK_PAD_DIM), BF16),
            jax.ShapeDtypeStruct((N_HEADS, s, QK_PAD_DIM), BF16),
            jax.ShapeDtypeStruct((N_HEADS, s, 2 * V_HEAD_DIM), BF16),
        ],
        compiler_params=_params(("parallel",)),
        name="lat",
    )(h, wlat, qg, kvg, wq, wkv, pos_col, freq)


def _gates_kernel(h_ref, w_ref, b_ref, o_ref, *, tn, d, pool_w):
    j = pl.program_id(1)
    z = jnp.dot(h_ref[...], w_ref[...], preferred_element_type=F32)
    col = j * tn
    is_ident = jnp.logical_and(col >= d, col < d + pool_w)
    is_sigm = col >= d + 2 * pool_w

    @pl.when(is_ident)
    def _():
        o_ref[...] = z.astype(o_ref.dtype)

    @pl.when(is_sigm)
    def _():
        o_ref[...] = jax.nn.sigmoid(z + b_ref[...]).astype(o_ref.dtype)

    @pl.when(jnp.logical_not(jnp.logical_or(is_ident, is_sigm)))
    def _():
        o_ref[...] = (z * jax.nn.sigmoid(z)).astype(o_ref.dtype)


def _gates(h, wg, bg, tm=1024, tn=1024):
    s, d = h.shape
    n = wg.shape[1]
    return pl.pallas_call(
        functools.partial(_gates_kernel, tn=tn, d=d, pool_w=POOL_WIDTH),
        grid=(s // tm, n // tn),
        in_specs=[
            pl.BlockSpec((tm, d), lambda i, j: (i, 0)),
            pl.BlockSpec((d, tn), lambda i, j: (0, j)),
            pl.BlockSpec((1, tn), lambda i, j: (0, j)),
        ],
        out_specs=pl.BlockSpec((tm, tn), lambda i, j: (i, j)),
        out_shape=jax.ShapeDtypeStruct((s, n), BF16),
        compiler_params=_params(("parallel", "arbitrary")),
        name="gates",
    )(h, wg, bg)


_NT_DIMS = (((1,), (1,)), ((), ()))


def _attn_kernel(q_ref, qn_ref, k_ref, v_ref, sg_ref, o_ref, sa_sc, sb_sc, m_sc, acc_sc,
                 *, bq, bk, rc):
    qi = pl.program_id(1)
    m_sc[...] = jnp.full(m_sc.shape, -jnp.inf, F32)
    acc_sc[...] = jnp.zeros(acc_sc.shape, F32)

    def qk(qv_ref, j, s_sc):
        start = pl.multiple_of(j * bk, bk)
        s_sc[...] = lax.dot_general(qv_ref[0], k_ref[0, pl.ds(start, bk), :], _NT_DIMS,
                                    preferred_element_type=F32)

    def softmax_pv(j, s_sc, diag):
        start = pl.multiple_of(j * bk, bk)
        for r in range(bq // rc):
            rows = slice(r * rc, (r + 1) * rc)
            if diag is None:
                ncol, masked = bk, False
            else:
                ncol = min(max((r + 1) * rc - diag * bk, 0), bk)
                masked = diag * bk + ncol - 1 > r * rc
            if ncol == 0:
                continue
            s = s_sc[rows, 0:ncol]
            if masked:
                row = r * rc + lax.broadcasted_iota(jnp.int32, s.shape, 0)
                colk = diag * bk + lax.broadcasted_iota(jnp.int32, s.shape, 1)
                s = jnp.where(colk <= row, s, -jnp.inf)
            m_prev = m_sc[rows, :]
            m_new = jnp.maximum(m_prev, jnp.max(s, axis=1, keepdims=True))
            alpha = jnp.exp2(m_prev - m_new)
            p = jnp.concatenate(
                [jnp.exp2(s[:, c * 128:(c + 1) * 128] - m_new) for c in range(ncol // 128)],
                axis=1).astype(BF16)
            v = v_ref[0, pl.ds(start, ncol), :]
            pv = jnp.dot(p, v, preferred_element_type=F32)
            acc_sc[rows, :] = jnp.concatenate([alpha, alpha], axis=1) * acc_sc[rows, :] + pv
            m_sc[rows, :] = m_new

    @pl.when(qi == 0)
    def _():
        qk(q_ref, 0, sa_sc)

    def pair(a):
        qk(q_ref, a + 1, sb_sc)
        softmax_pv(a, sa_sc, None)
        qk(q_ref, a + 2, sa_sc)
        softmax_pv(a + 1, sb_sc, None)

    def two_pairs(t, carry):
        pair(4 * t)
        pair(4 * t + 2)
        return carry

    lax.fori_loop(0, lax.shift_right_logical(qi, 1), two_pairs, 0)

    @pl.when(lax.bitwise_and(qi, 1) == 1)
    def _():
        pair(2 * qi - 2)

    a = 2 * qi
    qk(q_ref, a + 1, sb_sc)
    softmax_pv(a, sa_sc, 0)
    qk(qn_ref, 0, sa_sc)
    softmax_pv(a + 1, sb_sc, 1)

    acc = acc_sc[...]
    o = acc[:, 0:V_HEAD_DIM] / acc[:, V_HEAD_DIM:2 * V_HEAD_DIM]
    o_ref[...] = (o * sg_ref[...].astype(F32)).astype(o_ref.dtype)


def _attn(q, k, v1, g, bq=1024, rc=256):
    nh, s, _ = q.shape
    bk = bq // 2
    nq = s // bq
    return pl.pallas_call(
        functools.partial(_attn_kernel, bq=bq, bk=bk, rc=rc),
        grid=(nh, nq),
        in_specs=[
            pl.BlockSpec((1, bq, QK_PAD_DIM), lambda h, i: (h, i, 0)),
            pl.BlockSpec((1, bq, QK_PAD_DIM), lambda h, i: (h, jnp.minimum(i + 1, nq - 1), 0)),
            pl.BlockSpec((1, s, QK_PAD_DIM), lambda h, i: (h, 0, 0)),
            pl.BlockSpec((1, s, 2 * V_HEAD_DIM), lambda h, i: (h, 0, 0)),
            pl.BlockSpec((bq, V_HEAD_DIM), lambda h, i: (i, h)),
        ],
        out_specs=pl.BlockSpec((bq, V_HEAD_DIM), lambda h, i: (i, h)),
        out_shape=jax.ShapeDtypeStruct((s, nh * V_HEAD_DIM), BF16),
        scratch_shapes=[
            pltpu.VMEM((bq, bk), F32),
            pltpu.VMEM((bq, bk), F32),
            pltpu.VMEM((bq, 128), F32),
            pltpu.VMEM((bq, 2 * V_HEAD_DIM), F32),
        ],
        compiler_params=_params(("arbitrary", "arbitrary")),
        name="attn",
    )(q, q, k, v1, g)


def _pool_kernel(u_ref, halo_ref, pg_ref, w_ref, sc_ref, o_ref, *, tm):
    i = pl.program_id(0)
    u = u_ref[...].astype(F32)
    halo = jnp.where(i > 0, halo_ref[...].astype(F32), 0.0)
    ext = jnp.concatenate([halo, u], axis=0)
    t = i * tm + lax.broadcasted_iota(jnp.int32, (tm, 1), 0)
    n_ext = POOL_HALO + tm

    def shifted(a, kk):
        return jnp.concatenate([jnp.zeros((kk, a.shape[1]), F32), a[:a.shape[0] - kk]], axis=0)

    acc = ext
    width = 1
    for g, w in enumerate(POOL_WINDOWS):
        lo = g * POOL_GROUP_DIM
        acc = acc[:, (0 if g == 0 else POOL_GROUP_DIM):]
        while width < w:
            acc = acc + shifted(acc, width)
            width *= 2
        win = acc[POOL_HALO:n_ext, 0:POOL_GROUP_DIM]
        cnt = jnp.minimum(t + 1, w).astype(F32)
        pooled = win / cnt - u[:, lo:lo + POOL_GROUP_DIM]
        mixed = jnp.dot(pooled.astype(BF16), w_ref[g], preferred_element_type=F32)
        mixed = mixed * sc_ref[:, lo:lo + POOL_GROUP_DIM]
        o_ref[:, lo:lo + POOL_GROUP_DIM] = (
            mixed * pg_ref[:, lo:lo + POOL_GROUP_DIM].astype(F32)).astype(o_ref.dtype)


def _pool(g, w_pool, pool_scale, d, tm=512):
    s = g.shape[0]
    p = POOL_WIDTH
    ub = d // p
    hb = tm // POOL_HALO
    return pl.pallas_call(
        functools.partial(_pool_kernel, tm=tm),
        grid=(s // tm,),
        in_specs=[
            pl.BlockSpec((tm, p), lambda i: (i, ub)),
            pl.BlockSpec((POOL_HALO, p), lambda i: (jnp.maximum(i * hb - 1, 0), ub)),
            pl.BlockSpec((tm, p), lambda i: (i, ub + 1)),
            _const_spec(w_pool.shape),
            _const_spec(pool_scale.shape),
        ],
        out_specs=pl.BlockSpec((tm, p), lambda i: (i, 0)),
        out_shape=jax.ShapeDtypeStruct((s, p), BF16),
        compiler_params=_params(("parallel",)),
        name="pool",
    )(g, g, g, w_pool, pool_scale)


def _final_kernel(ya_ref, yp_ref, gm_ref, gp_ref, x_ref, mod_ref, lng_ref, lnb_ref,
                  wmo_ref, wpo_ref, wout_ref, o_ref, *, d):
    y_mla = jnp.dot(ya_ref[...], wmo_ref[...], preferred_element_type=F32)
    y_pool = jnp.dot(yp_ref[...], wpo_ref[...], preferred_element_type=F32)
    merged = gm_ref[...].astype(F32) * y_mla + gp_ref[...].astype(F32) * y_pool
    y = jnp.dot(merged.astype(BF16), wout_ref[...], preferred_element_type=F32)
    gate = mod_ref[:, 2 * d:3 * d]
    r = DEEPNORM_ALPHA * x_ref[...] + (1.0 + gate) * y
    mu = jnp.mean(r, axis=-1, keepdims=True)
    rc = r - mu
    var = jnp.mean(rc * rc, axis=-1, keepdims=True)
    o_ref[...] = rc * lax.rsqrt(var + LN_EPS) * lng_ref[...] + lnb_ref[...]


def _final(ya, yp, g, x2, mod, ln_g, ln_b, wmo, wpo, wout, tm=512):
    s, d = x2.shape
    gb = (d + 2 * POOL_WIDTH) // d
    return pl.pallas_call(
        functools.partial(_final_kernel, d=d),
        grid=(s // tm,),
        in_specs=[
            pl.BlockSpec((tm, ya.shape[1]), lambda i: (i, 0)),
            pl.BlockSpec((tm, yp.shape[1]), lambda i: (i, 0)),
            pl.BlockSpec((tm, d), lambda i: (i, gb)),
            pl.BlockSpec((tm, d), lambda i: (i, gb + 1)),
            pl.BlockSpec((tm, d), lambda i: (i, 0)),
            _const_spec(mod.shape),
            _const_spec(ln_g.shape),
            _const_spec(ln_b.shape),
            _const_spec(wmo.shape),
            _const_spec(wpo.shape),
            _const_spec(wout.shape),
        ],
        out_specs=pl.BlockSpec((tm, d), lambda i: (i, 0)),
        out_shape=jax.ShapeDtypeStruct((s, d), F32),
        compiler_params=_params(("parallel",)),
        name="final",
    )(ya, yp, g, g, x2, mod, ln_g, ln_b, wmo, wpo, wout)


def _swap_halves(w):
    half = QK_ROPE_DIM // 2
    return jnp.concatenate([w[..., half:], w[..., :half]], axis=-1)


def kernel(x, c, positions, w_ada, b_ada, w_in, b_gates, q_norm_g, w_q_b, kv_norm_g, w_kv_b,
           w_mla_o, w_pool_g, pool_scale, w_pool_o, w_out, ln_g, ln_b):
    b, s, d = x.shape
    assert b == 1 and w_ada.shape[0] == 1
    l = 0
    x2 = x.reshape(s, d)

    o_q, o_kv = Q_LORA_RANK, Q_LORA_RANK + KV_LORA_RANK
    o_g = o_kv + QK_ROPE_DIM
    w_kpe = w_in[l][:, o_kv:o_g]
    wlat = jnp.concatenate([w_in[l][:, :o_kv], w_kpe, _swap_halves(w_kpe)], axis=1).astype(BF16)
    wg = w_in[l][:, o_g:].astype(BF16)
    bg = jnp.concatenate([jnp.zeros((d + 2 * POOL_WIDTH,), F32), b_gates[l]]).reshape(1, -1)
    wq3 = w_q_b[l].reshape(Q_LORA_RANK, N_HEADS, QK_HEAD_DIM)
    wq_pe = wq3[..., QK_NOPE_DIM:]
    wq = jnp.concatenate([wq3[..., :QK_NOPE_DIM], wq_pe, _swap_halves(wq_pe)], axis=-1)
    wq = wq.reshape(Q_LORA_RANK, N_HEADS * QK_PAD_DIM).astype(BF16)
    wkv = w_kv_b[l].astype(BF16)
    inv_freq = ROPE_THETA ** (-jnp.arange(0, QK_ROPE_DIM, 2, dtype=F32) / QK_ROPE_DIM)
    freq = jnp.tile(inv_freq, 4).reshape(1, 128)
    pos_col = positions.reshape(s, 1)

    mod = _mod(c.reshape(d, 1), w_ada[l], b_ada[l].reshape(1, -1))
    h = _ln(x2, mod)
    q, k, v = _lat(h, wlat, q_norm_g[l].reshape(1, -1), kv_norm_g[l].reshape(1, -1),
                   wq, wkv, pos_col, freq)
    g = _gates(h, wg, bg)
    ya = _attn(q, k, v, g)
    yp = _pool(g, w_pool_g[l].astype(BF16), pool_scale[l].reshape(1, -1), d)
    out = _final(ya, yp, g, x2, mod, ln_g[l].reshape(1, -1), ln_b[l].reshape(1, -1),
                 w_mla_o[l].astype(BF16), w_pool_o[l].astype(BF16), w_out[l].astype(BF16))
    return out.reshape(b, s, d)
```

```python
import functools
import math

import jax
import jax.numpy as jnp
from jax import lax
from jax.experimental import pallas as pl
from jax.experimental.pallas import tpu as pltpu

N_HEADS = 16
QK_NOPE_DIM = 128
QK_ROPE_DIM = 64
QK_HEAD_DIM = QK_NOPE_DIM + QK_ROPE_DIM
V_HEAD_DIM = 128
Q_LORA_RANK = 512
KV_LORA_RANK = 512
ROPE_THETA = 10000.0
POOL_WINDOWS = (2, 4, 8, 16)
POOL_GROUP_DIM = 256
POOL_WIDTH = len(POOL_WINDOWS) * POOL_GROUP_DIM
RMS_EPS = 1e-6
LN_EPS = 1e-5
DEPTH = 1
DEEPNORM_ALPHA = (2.0 * DEPTH) ** 0.25

QK_PAD_DIM = 256
VT_ROWS = V_HEAD_DIM + 16
POOL_HALO = 16
VMEM_LIMIT = 56 * 1024 * 1024

F32 = jnp.float32
BF16 = jnp.bfloat16


def _const_spec(shape):
    nd = len(shape)
    return pl.BlockSpec(shape, lambda *_: (0,) * nd, pipeline_mode=pl.Buffered(1))


def _params(sem):
    return pltpu.CompilerParams(dimension_semantics=sem, vmem_limit_bytes=VMEM_LIMIT)


def _mod_kernel(c_ref, w_ref, b_ref, o_ref):
    c = c_ref[...]
    sc = c * jax.nn.sigmoid(c)
    o_ref[...] = jnp.sum(sc * w_ref[...], axis=0, keepdims=True) + b_ref[...]


def _mod(c_col, w_ada, b_ada, tn=512):
    d, n = w_ada.shape
    return pl.pallas_call(
        _mod_kernel,
        grid=(n // tn,),
        in_specs=[
            pl.BlockSpec((d, 1), lambda j: (0, 0)),
            pl.BlockSpec((d, tn), lambda j: (0, j)),
            pl.BlockSpec((1, tn), lambda j: (0, j)),
        ],
        out_specs=pl.BlockSpec((1, tn), lambda j: (0, j)),
        out_shape=jax.ShapeDtypeStruct((1, n), F32),
        compiler_params=_params(("arbitrary",)),
        name="mod",
    )(c_col, w_ada, b_ada)


def _ln_kernel(x_ref, mod_ref, h_ref, *, d):
    x = x_ref[...]
    mu = jnp.mean(x, axis=-1, keepdims=True)
    xc = x - mu
    var = jnp.mean(xc * xc, axis=-1, keepdims=True)
    y = xc * lax.rsqrt(var + LN_EPS)
    shift = mod_ref[:, 0:d]
    scale = mod_ref[:, d:2 * d]
    h_ref[...] = (y * (1.0 + scale) + shift).astype(h_ref.dtype)


def _ln(x2, mod, tm=512):
    s, d = x2.shape
    return pl.pallas_call(
        functools.partial(_ln_kernel, d=d),
        grid=(s // tm,),
        in_specs=[
            pl.BlockSpec((tm, d), lambda i: (i, 0)),
            pl.BlockSpec((1, 3 * d), lambda i: (0, 0)),
        ],
        out_specs=pl.BlockSpec((tm, d), lambda i: (i, 0)),
        out_shape=jax.ShapeDtypeStruct((s, d), BF16),
        compiler_params=_params(("parallel",)),
        name="ln",
    )(x2, mod)


def _rms(xf, g):
    return xf * lax.rsqrt(jnp.mean(xf * xf, axis=-1, keepdims=True) + RMS_EPS) * g


def _lat_kernel(h_ref, wlat_ref, qg_ref, kvg_ref, wqt_ref, wk_ref, wvt_ref, pos_ref, freq_ref,
                qt_ref, k_ref, vt_ref, *, q_scale):
    h = h_ref[...]
    lat = jnp.dot(h, wlat_ref[...], preferred_element_type=F32)
    qn = _rms(lat[:, 0:Q_LORA_RANK], qg_ref[...])
    cn = _rms(lat[:, Q_LORA_RANK:Q_LORA_RANK + KV_LORA_RANK], kvg_ref[...])
    kpe = lat[:, Q_LORA_RANK + KV_LORA_RANK:]
    qn_t = qn.T.astype(BF16)
    cn_t = cn.T.astype(BF16)
    cn = cn.astype(BF16)

    ang = pos_ref[...].astype(F32) * freq_ref[...]
    lane = lax.broadcasted_iota(jnp.int32, ang.shape, 1)
    cs = jnp.where(lane < 64, jnp.cos(ang),
                   jnp.where(lane < 96, -jnp.sin(ang), jnp.sin(ang)))
    cs_t = cs.T

    r = kpe * cs
    k_rot = jnp.where(lane < 64, r + pltpu.roll(r, 64, 1), 0.0).astype(BF16)
    zeros_t = jnp.zeros((QK_PAD_DIM - QK_HEAD_DIM, qn_t.shape[1]), BF16)
    ones_t = jnp.ones((VT_ROWS - V_HEAD_DIM, qn_t.shape[1]), BF16)
    for hd in range(N_HEADS):
        qh = jnp.dot(wqt_ref[hd], qn_t, preferred_element_type=F32)
        rq = qh[128:256, :] * cs_t
        qt_ref[hd, 0:128, :] = (qh[0:128, :] * q_scale).astype(BF16)
        qt_ref[hd, 128:192, :] = ((rq[0:64, :] + rq[64:128, :]) * q_scale).astype(BF16)
        qt_ref[hd, 192:256, :] = zeros_t
    for hp in range(N_HEADS // 2):
        kk = jnp.dot(cn, wk_ref[:, hp * 256:(hp + 1) * 256], preferred_element_type=F32)
        vv = jnp.dot(wvt_ref[hp * 256:(hp + 1) * 256, :], cn_t, preferred_element_type=F32)
        for e in range(2):
            hd = 2 * hp + e
            k_ref[hd, :, 0:128] = kk[:, e * 128:(e + 1) * 128].astype(BF16)
            k_ref[hd, :, 128:256] = k_rot
            vt_ref[hd, 0:V_HEAD_DIM, :] = vv[e * 128:(e + 1) * 128, :].astype(BF16)
            vt_ref[hd, V_HEAD_DIM:VT_ROWS, :] = ones_t


def _lat(h, wlat, qg, kvg, wqt, wk, wvt, pos_col, freq, tm=256):
    s, d = h.shape
    q_scale = (QK_HEAD_DIM ** -0.5) * math.log2(math.e)
    return pl.pallas_call(
        functools.partial(_lat_kernel, q_scale=q_scale),
        grid=(s // tm,),
        in_specs=[
            pl.BlockSpec((tm, d), lambda i: (i, 0)),
            _const_spec(wlat.shape),
            _const_spec(qg.shape),
            _const_spec(kvg.shape),
            _const_spec(wqt.shape),
            _const_spec(wk.shape),
            _const_spec(wvt.shape),
            pl.BlockSpec((tm, 1), lambda i: (i, 0)),
            _const_spec(freq.shape),
        ],
        out_specs=[
            pl.BlockSpec((N_HEADS, QK_PAD_DIM, tm), lambda i: (0, 0, i)),
            pl.BlockSpec((N_HEADS, tm, QK_PAD_DIM), lambda i: (0, i, 0)),
            pl.BlockSpec((N_HEADS, VT_ROWS, tm), lambda i: (0, 0, i)),
        ],
        out_shape=[
            jax.ShapeDtypeStruct((N_HEADS, QK_PAD_DIM, s), BF16),
            jax.ShapeDtypeStruct((N_HEADS, s, QK_PAD_DIM), BF16),
            jax.ShapeDtypeStruct((N_HEADS, VT_ROWS, s), BF16),
        ],
        compiler_params=_params(("parallel",)),
        name="lat",
    )(h, wlat, qg, kvg, wqt, wk, wvt, pos_col, freq)


def _gates_kernel(h_ref, w_ref, b_ref, o_ref, *, tn, cn, d, pool_w):
    j = pl.program_id(1)
    h = h_ref[...]
    for c in range(tn // cn):
        cols = slice(c * cn, (c + 1) * cn)
        col = j * tn + c * cn
        is_ident = jnp.logical_and(col >= d, col < d + pool_w)
        is_sigm = col >= d + 2 * pool_w
        z = jnp.dot(h, w_ref[:, cols], preferred_element_type=F32)
        sg = jax.nn.sigmoid(z + b_ref[:, cols])
        o_ref[:, cols] = jnp.where(is_ident, z, jnp.where(is_sigm, sg, z * sg)).astype(o_ref.dtype)


def _gates(h, wg, bg, tm=1024, tn=2048, cn=512):
    s, d = h.shape
    n = wg.shape[1]
    return pl.pallas_call(
        functools.partial(_gates_kernel, tn=tn, cn=cn, d=d, pool_w=POOL_WIDTH),
        grid=(s // tm, n // tn),
        in_specs=[
            pl.BlockSpec((tm, d), lambda i, j: (i, 0)),
            pl.BlockSpec((d, tn), lambda i, j: (0, j)),
            pl.BlockSpec((1, tn), lambda i, j: (0, j)),
        ],
        out_specs=pl.BlockSpec((tm, tn), lambda i, j: (i, j)),
        out_shape=jax.ShapeDtypeStruct((s, n), BF16),
        compiler_params=_params(("parallel", "arbitrary")),
        name="gates",
    )(h, wg, bg)


def _attn_kernel(qt_ref, qtn_ref, k_ref, vt_ref, sg_ref, o_ref, sa_sc, sb_sc, m_sc, acc_sc,
                 *, bq, bk, cw):
    qi = pl.program_id(1)
    m_sc[...] = jnp.full(m_sc.shape, -jnp.inf, F32)
    acc_sc[...] = jnp.zeros(acc_sc.shape, F32)

    def qk(qv_ref, j, s_sc):
        start = pl.multiple_of(j * bk, bk)
        s_sc[...] = jnp.dot(k_ref[0, pl.ds(start, bk), :], qv_ref[0],
                            preferred_element_type=F32)

    def softmax_pv(j, s_sc, diag):
        start = pl.multiple_of(j * bk, bk)
        for c in range(bq // cw):
            cols = slice(c * cw, (c + 1) * cw)
            if diag is None:
                nrow, masked = bk, False
            else:
                nrow = min(max((c + 1) * cw - diag * bk, 0), bk)
                masked = diag * bk + nrow - 1 > c * cw
            if nrow == 0:
                continue
            s = s_sc[0:nrow, cols]
            if masked:
                key = diag * bk + lax.broadcasted_iota(jnp.int32, s.shape, 0)
                qry = c * cw + lax.broadcasted_iota(jnp.int32, s.shape, 1)
                s = jnp.where(key <= qry, s, -jnp.inf)
            m_prev = m_sc[:, cols]
            m_new = jnp.maximum(m_prev, jnp.max(s, axis=0, keepdims=True))
            alpha = jnp.exp2(m_prev - m_new)
            p = jnp.exp2(s - m_new[0:1, :])
            vt = vt_ref[0, :, pl.ds(start, nrow)]
            pv = jnp.dot(vt, p.astype(BF16), preferred_element_type=F32)
            acc_sc[:, cols] = alpha[0:1, :] * acc_sc[:, cols] + pv
            m_sc[:, cols] = m_new

    @pl.when(qi == 0)
    def _():
        qk(qt_ref, 0, sa_sc)

    def pair(a):
        qk(qt_ref, a + 1, sb_sc)
        softmax_pv(a, sa_sc, None)
        qk(qt_ref, a + 2, sa_sc)
        softmax_pv(a + 1, sb_sc, None)

    def two_pairs(t, carry):
        pair(4 * t)
        pair(4 * t + 2)
        return carry

    lax.fori_loop(0, lax.shift_right_logical(qi, 1), two_pairs, 0)

    @pl.when(lax.bitwise_and(qi, 1) == 1)
    def _():
        pair(2 * qi - 2)

    a = 2 * qi
    qk(qt_ref, a + 1, sb_sc)
    softmax_pv(a, sa_sc, 0)
    qk(qtn_ref, 0, sa_sc)
    softmax_pv(a + 1, sb_sc, 1)

    o_t = acc_sc[0:V_HEAD_DIM, :] / acc_sc[V_HEAD_DIM:V_HEAD_DIM + 1, :]
    o_ref[...] = (o_t.T * sg_ref[...].astype(F32)).astype(o_ref.dtype)


def _attn(qt, k, vt, g, bq=1024, cw=256):
    nh, _, s = qt.shape
    bk = bq // 2
    nq = s // bq
    return pl.pallas_call(
        functools.partial(_attn_kernel, bq=bq, bk=bk, cw=cw),
        grid=(nh, nq),
        in_specs=[
            pl.BlockSpec((1, QK_PAD_DIM, bq), lambda h, i: (h, 0, i)),
            pl.BlockSpec((1, QK_PAD_DIM, bq), lambda h, i: (h, 0, jnp.minimum(i + 1, nq - 1))),
            pl.BlockSpec((1, s, QK_PAD_DIM), lambda h, i: (h, 0, 0)),
            pl.BlockSpec((1, VT_ROWS, s), lambda h, i: (h, 0, 0)),
            pl.BlockSpec((bq, V_HEAD_DIM), lambda h, i: (i, h)),
        ],
        out_specs=pl.BlockSpec((bq, V_HEAD_DIM), lambda h, i: (i, h)),
        out_shape=jax.ShapeDtypeStruct((s, nh * V_HEAD_DIM), BF16),
        scratch_shapes=[
            pltpu.VMEM((bk, bq), F32),
            pltpu.VMEM((bk, bq), F32),
            pltpu.VMEM((8, bq), F32),
            pltpu.VMEM((VT_ROWS, bq), F32),
        ],
        compiler_params=_params(("arbitrary", "arbitrary")),
        name="attn",
    )(qt, qt, k, vt, g)


def _pool_kernel(u_ref, halo_ref, pg_ref, w_ref, sc_ref, o_ref, *, tm):
    i = pl.program_id(0)
    u = u_ref[...].astype(F32)
    halo = jnp.where(i > 0, halo_ref[...].astype(F32), 0.0)
    ext = jnp.concatenate([halo, u], axis=0)
    t = i * tm + lax.broadcasted_iota(jnp.int32, (tm, 1), 0)
    n_ext = POOL_HALO + tm

    def shifted(a, kk):
        return jnp.concatenate([jnp.zeros((kk, a.shape[1]), F32), a[:a.shape[0] - kk]], axis=0)

    acc = ext
    width = 1
    for g, w in enumerate(POOL_WINDOWS):
        lo = g * POOL_GROUP_DIM
        acc = acc[:, (0 if g == 0 else POOL_GROUP_DIM):]
        while width < w:
            acc = acc + shifted(acc, width)
            width *= 2
        win = acc[POOL_HALO:n_ext, 0:POOL_GROUP_DIM]
        cnt = jnp.minimum(t + 1, w).astype(F32)
        pooled = win / cnt - u[:, lo:lo + POOL_GROUP_DIM]
        mixed = jnp.dot(pooled.astype(BF16), w_ref[g], preferred_element_type=F32)
        mixed = mixed * sc_ref[:, lo:lo + POOL_GROUP_DIM]
        o_ref[:, lo:lo + POOL_GROUP_DIM] = (
            mixed * pg_ref[:, lo:lo + POOL_GROUP_DIM].astype(F32)).astype(o_ref.dtype)


def _pool(g, w_pool, pool_scale, d, tm=512):
    s = g.shape[0]
    p = POOL_WIDTH
    ub = d // p
    hb = tm // POOL_HALO
    return pl.pallas_call(
        functools.partial(_pool_kernel, tm=tm),
        grid=(s // tm,),
        in_specs=[
            pl.BlockSpec((tm, p), lambda i: (i, ub)),
            pl.BlockSpec((POOL_HALO, p), lambda i: (jnp.maximum(i * hb - 1, 0), ub)),
            pl.BlockSpec((tm, p), lambda i: (i, ub + 1)),
            _const_spec(w_pool.shape),
            _const_spec(pool_scale.shape),
        ],
        out_specs=pl.BlockSpec((tm, p), lambda i: (i, 0)),
        out_shape=jax.ShapeDtypeStruct((s, p), BF16),
        compiler_params=_params(("parallel",)),
        name="pool",
    )(g, g, g, w_pool, pool_scale)


def _final_kernel(ya_ref, yp_ref, gm_ref, gp_ref, x_ref, mod_ref, lng_ref, lnb_ref,
                  wmo_ref, wpo_ref, wout_ref, o_ref, *, d):
    y_mla = jnp.dot(ya_ref[...], wmo_ref[...], preferred_element_type=F32)
    y_pool = jnp.dot(yp_ref[...], wpo_ref[...], preferred_element_type=F32)
    merged = gm_ref[...].astype(F32) * y_mla + gp_ref[...].astype(F32) * y_pool
    y = jnp.dot(merged.astype(BF16), wout_ref[...], preferred_element_type=F32)
    gate = mod_ref[:, 2 * d:3 * d]
    r = DEEPNORM_ALPHA * x_ref[...] + (1.0 + gate) * y
    mu = jnp.mean(r, axis=-1, keepdims=True)
    rc = r - mu
    var = jnp.mean(rc * rc, axis=-1, keepdims=True)
    o_ref[...] = rc * lax.rsqrt(var + LN_EPS) * lng_ref[...] + lnb_ref[...]


def _final(ya, yp, g, x2, mod, ln_g, ln_b, wmo, wpo, wout, tm=512):
    s, d = x2.shape
    gb = (d + 2 * POOL_WIDTH) // d
    return pl.pallas_call(
        functools.partial(_final_kernel, d=d),
        grid=(s // tm,),
        in_specs=[
            pl.BlockSpec((tm, ya.shape[1]), lambda i: (i, 0)),
            pl.BlockSpec((tm, yp.shape[1]), lambda i: (i, 0)),
            pl.BlockSpec((tm, d), lambda i: (i, gb)),
            pl.BlockSpec((tm, d), lambda i: (i, gb + 1)),
            pl.BlockSpec((tm, d), lambda i: (i, 0)),
            _const_spec(mod.shape),
            _const_spec(ln_g.shape),
            _const_spec(ln_b.shape),
            _const_spec(wmo.shape),
            _const_spec(wpo.shape),
            _const_spec(wout.shape),
        ],
        out_specs=pl.BlockSpec((tm, d), lambda i: (i, 0)),
        out_shape=jax.ShapeDtypeStruct((s, d), F32),
        compiler_params=_params(("parallel",)),
        name="final",
    )(ya, yp, g, g, x2, mod, ln_g, ln_b, wmo, wpo, wout)


def _swap_halves(w):
    half = QK_ROPE_DIM // 2
    return jnp.concatenate([w[..., half:], w[..., :half]], axis=-1)


def kernel(x, c, positions, w_ada, b_ada, w_in, b_gates, q_norm_g, w_q_b, kv_norm_g, w_kv_b,
           w_mla_o, w_pool_g, pool_scale, w_pool_o, w_out, ln_g, ln_b):
    b, s, d = x.shape
    assert b == 1 and w_ada.shape[0] == 1
    l = 0
    x2 = x.reshape(s, d)

    o_q, o_kv = Q_LORA_RANK, Q_LORA_RANK + KV_LORA_RANK
    o_g = o_kv + QK_ROPE_DIM
    w_kpe = w_in[l][:, o_kv:o_g]
    wlat = jnp.concatenate([w_in[l][:, :o_kv], w_kpe, _swap_halves(w_kpe)], axis=1).astype(BF16)
    wg = w_in[l][:, o_g:].astype(BF16)
    bg = jnp.concatenate([jnp.zeros((d + 2 * POOL_WIDTH,), F32), b_gates[l]]).reshape(1, -1)
    wq3 = w_q_b[l].reshape(Q_LORA_RANK, N_HEADS, QK_HEAD_DIM)
    wq_pe = wq3[..., QK_NOPE_DIM:]
    wq = jnp.concatenate([wq3[..., :QK_NOPE_DIM], wq_pe, _swap_halves(wq_pe)], axis=-1)
    wqt = jnp.transpose(wq, (1, 2, 0)).astype(BF16)
    wkv3 = w_kv_b[l].reshape(KV_LORA_RANK, N_HEADS, QK_NOPE_DIM + V_HEAD_DIM)
    wk = wkv3[..., :QK_NOPE_DIM].reshape(KV_LORA_RANK, N_HEADS * QK_NOPE_DIM).astype(BF16)
    wvt = jnp.transpose(wkv3[..., QK_NOPE_DIM:], (1, 2, 0)).reshape(
        N_HEADS * V_HEAD_DIM, KV_LORA_RANK).astype(BF16)
    inv_freq = ROPE_THETA ** (-jnp.arange(0, QK_ROPE_DIM, 2, dtype=F32) / QK_ROPE_DIM)
    freq = jnp.tile(inv_freq, 4).reshape(1, 128)
    pos_col = positions.reshape(s, 1)

    mod = _mod(c.reshape(d, 1), w_ada[l], b_ada[l].reshape(1, -1))
    h = _ln(x2, mod)
    qt, k, vt = _lat(h, wlat, q_norm_g[l].reshape(1, -1), kv_norm_g[l].reshape(1, -1),
                     wqt, wk, wvt, pos_col, freq)
    g = _gates(h, wg, bg)
    ya = _attn(qt, k, vt, g)
    yp = _pool(g, w_pool_g[l].astype(BF16), pool_scale[l].reshape(1, -1), d)
    out = _final(ya, yp, g, x2, mod, ln_g[l].reshape(1, -1), ln_b[l].reshape(1, -1),
                 w_mla_o[l].astype(BF16), w_pool_o[l].astype(BF16), w_out[l].astype(BF16))
    return out.reshape(b, s, d)
```

```python
import functools
import math

import jax
import jax.numpy as jnp
from jax import lax
from jax.experimental import pallas as pl
from jax.experimental.pallas import tpu as pltpu

N_HEADS = 16
QK_NOPE_DIM = 128
QK_ROPE_DIM = 64
QK_HEAD_DIM = QK_NOPE_DIM + QK_ROPE_DIM
V_HEAD_DIM = 128
Q_LORA_RANK = 512
KV_LORA_RANK = 512
ROPE_THETA = 10000.0
POOL_WINDOWS = (2, 4, 8, 16)
POOL_GROUP_DIM = 256
POOL_WIDTH = len(POOL_WINDOWS) * POOL_GROUP_DIM
RMS_EPS = 1e-6
LN_EPS = 1e-5
DEPTH = 1
DEEPNORM_ALPHA = (2.0 * DEPTH) ** 0.25

QK_PAD_DIM = 256
VT_ROWS = V_HEAD_DIM + 16
POOL_HALO = 16
VMEM_LIMIT = 56 * 1024 * 1024

F32 = jnp.float32
BF16 = jnp.bfloat16


def _const_spec(shape):
    nd = len(shape)
    return pl.BlockSpec(shape, lambda *_: (0,) * nd, pipeline_mode=pl.Buffered(1))


def _params(sem):
    return pltpu.CompilerParams(dimension_semantics=sem, vmem_limit_bytes=VMEM_LIMIT)


def _mod_kernel(c_ref, w_ref, b_ref, o_ref):
    c = c_ref[...]
    sc = c * jax.nn.sigmoid(c)
    o_ref[...] = jnp.sum(sc * w_ref[...], axis=0, keepdims=True) + b_ref[...]


def _mod(c_col, w_ada, b_ada, tn=512):
    d, n = w_ada.shape
    return pl.pallas_call(
        _mod_kernel,
        grid=(n // tn,),
        in_specs=[
            pl.BlockSpec((d, 1), lambda j: (0, 0)),
            pl.BlockSpec((d, tn), lambda j: (0, j)),
            pl.BlockSpec((1, tn), lambda j: (0, j)),
        ],
        out_specs=pl.BlockSpec((1, tn), lambda j: (0, j)),
        out_shape=jax.ShapeDtypeStruct((1, n), F32),
        compiler_params=_params(("arbitrary",)),
        name="mod",
    )(c_col, w_ada, b_ada)


def _ln_kernel(x_ref, mod_ref, h_ref, *, d):
    x = x_ref[...]
    mu = jnp.mean(x, axis=-1, keepdims=True)
    xc = x - mu
    var = jnp.mean(xc * xc, axis=-1, keepdims=True)
    y = xc * lax.rsqrt(var + LN_EPS)
    shift = mod_ref[:, 0:d]
    scale = mod_ref[:, d:2 * d]
    h_ref[...] = (y * (1.0 + scale) + shift).astype(h_ref.dtype)


def _ln(x2, mod, tm=512):
    s, d = x2.shape
    return pl.pallas_call(
        functools.partial(_ln_kernel, d=d),
        grid=(s // tm,),
        in_specs=[
            pl.BlockSpec((tm, d), lambda i: (i, 0)),
            pl.BlockSpec((1, 3 * d), lambda i: (0, 0)),
        ],
        out_specs=pl.BlockSpec((tm, d), lambda i: (i, 0)),
        out_shape=jax.ShapeDtypeStruct((s, d), BF16),
        compiler_params=_params(("parallel",)),
        name="ln",
    )(x2, mod)


def _rms(xf, g):
    return xf * lax.rsqrt(jnp.mean(xf * xf, axis=-1, keepdims=True) + RMS_EPS) * g


def _lat_kernel(h_ref, wlat_ref, qg_ref, kvg_ref, wqt_ref, wk_ref, wvt_ref, pos_ref, freq_ref,
                qt_ref, k_ref, vt_ref, *, q_scale):
    h = h_ref[...]
    lat = jnp.dot(h, wlat_ref[...], preferred_element_type=F32)
    qn = _rms(lat[:, 0:Q_LORA_RANK], qg_ref[...])
    cn = _rms(lat[:, Q_LORA_RANK:Q_LORA_RANK + KV_LORA_RANK], kvg_ref[...])
    kpe = lat[:, Q_LORA_RANK + KV_LORA_RANK:]
    qn_t = qn.T.astype(BF16)
    cn_t = cn.T.astype(BF16)
    cn = cn.astype(BF16)

    ang = pos_ref[...].astype(F32) * freq_ref[...]
    lane = lax.broadcasted_iota(jnp.int32, ang.shape, 1)
    cs = jnp.where(lane < 64, jnp.cos(ang),
                   jnp.where(lane < 96, -jnp.sin(ang), jnp.sin(ang)))
    cs_t = cs.T

    r = kpe * cs
    k_rot = jnp.where(lane < 64, r + pltpu.roll(r, 64, 1), 0.0).astype(BF16)
    zeros_t = jnp.zeros((QK_PAD_DIM - QK_HEAD_DIM, qn_t.shape[1]), BF16)
    ones_t = jnp.ones((VT_ROWS - V_HEAD_DIM, qn_t.shape[1]), BF16)
    for hd in range(N_HEADS):
        qh = jnp.dot(wqt_ref[hd], qn_t, preferred_element_type=F32)
        rq = qh[128:256, :] * cs_t
        qt_ref[hd, 0:128, :] = (qh[0:128, :] * q_scale).astype(BF16)
        qt_ref[hd, 128:192, :] = ((rq[0:64, :] + rq[64:128, :]) * q_scale).astype(BF16)
        qt_ref[hd, 192:256, :] = zeros_t
    for hp in range(N_HEADS // 2):
        kk = jnp.dot(cn, wk_ref[:, hp * 256:(hp + 1) * 256], preferred_element_type=F32)
        vv = jnp.dot(wvt_ref[hp * 256:(hp + 1) * 256, :], cn_t, preferred_element_type=F32)
        for e in range(2):
            hd = 2 * hp + e
            k_ref[hd, :, 0:128] = kk[:, e * 128:(e + 1) * 128].astype(BF16)
            k_ref[hd, :, 128:256] = k_rot
            vt_ref[hd, 0:V_HEAD_DIM, :] = vv[e * 128:(e + 1) * 128, :].astype(BF16)
            vt_ref[hd, V_HEAD_DIM:VT_ROWS, :] = ones_t


def _lat(h, wlat, qg, kvg, wqt, wk, wvt, pos_col, freq, tm=256):
    s, d = h.shape
    q_scale = (QK_HEAD_DIM ** -0.5) * math.log2(math.e)
    return pl.pallas_call(
        functools.partial(_lat_kernel, q_scale=q_scale),
        grid=(s // tm,),
        in_specs=[
            pl.BlockSpec((tm, d), lambda i: (i, 0)),
            _const_spec(wlat.shape),
            _const_spec(qg.shape),
            _const_spec(kvg.shape),
            _const_spec(wqt.shape),
            _const_spec(wk.shape),
            _const_spec(wvt.shape),
            pl.BlockSpec((tm, 1), lambda i: (i, 0)),
            _const_spec(freq.shape),
        ],
        out_specs=[
            pl.BlockSpec((N_HEADS, QK_PAD_DIM, tm), lambda i: (0, 0, i)),
            pl.BlockSpec((N_HEADS, tm, QK_PAD_DIM), lambda i: (0, i, 0)),
            pl.BlockSpec((N_HEADS, VT_ROWS, tm), lambda i: (0, 0, i)),
        ],
        out_shape=[
            jax.ShapeDtypeStruct((N_HEADS, QK_PAD_DIM, s), BF16),
            jax.ShapeDtypeStruct((N_HEADS, s, QK_PAD_DIM), BF16),
            jax.ShapeDtypeStruct((N_HEADS, VT_ROWS, s), BF16),
        ],
        compiler_params=_params(("parallel",)),
        name="lat",
    )(h, wlat, qg, kvg, wqt, wk, wvt, pos_col, freq)


def _gates_kernel(h_ref, w_ref, b_ref, o_ref, *, tn, cn, d, pool_w):
    j = pl.program_id(1)
    h = h_ref[...]
    for c in range(tn // cn):
        cols = slice(c * cn, (c + 1) * cn)
        col = j * tn + c * cn
        is_ident = jnp.logical_and(col >= d, col < d + pool_w)
        is_sigm = col >= d + 2 * pool_w
        z = jnp.dot(h, w_ref[:, cols], preferred_element_type=F32)
        sg = jax.nn.sigmoid(z + b_ref[:, cols])
        o_ref[:, cols] = jnp.where(is_ident, z, jnp.where(is_sigm, sg, z * sg)).astype(o_ref.dtype)


def _gates(h, wg, bg, tm=1024, tn=2048, cn=512):
    s, d = h.shape
    n = wg.shape[1]
    return pl.pallas_call(
        functools.partial(_gates_kernel, tn=tn, cn=cn, d=d, pool_w=POOL_WIDTH),
        grid=(s // tm, n // tn),
        in_specs=[
            pl.BlockSpec((tm, d), lambda i, j: (i, 0)),
            pl.BlockSpec((d, tn), lambda i, j: (0, j)),
            pl.BlockSpec((1, tn), lambda i, j: (0, j)),
        ],
        out_specs=pl.BlockSpec((tm, tn), lambda i, j: (i, j)),
        out_shape=jax.ShapeDtypeStruct((s, n), BF16),
        compiler_params=_params(("parallel", "arbitrary")),
        name="gates",
    )(h, wg, bg)


def _attn_kernel(qt_ref, k_ref, vt_ref, sg_ref, o_ref, sa_sc, sb_sc, m_sc, acc_sc,
                 *, bq, bk, cw, nq):
    qi = pl.program_id(1)
    q_cur = pl.multiple_of(qi * bq, bq)
    q_nxt = pl.multiple_of(jnp.minimum(qi + 1, nq - 1) * bq, bq)
    m_sc[...] = jnp.full(m_sc.shape, -jnp.inf, F32)
    acc_sc[...] = jnp.zeros(acc_sc.shape, F32)

    def qk(q_start, j, s_sc, first_col=0):
        start = pl.multiple_of(j * bk, bk)
        q_t = qt_ref[0, :, pl.ds(pl.multiple_of(q_start + first_col, cw), bq - first_col)]
        s_sc[:, first_col:bq] = jnp.dot(k_ref[0, pl.ds(start, bk), :], q_t,
                                        preferred_element_type=F32)

    def softmax_pv(j, s_sc, diag):
        start = pl.multiple_of(j * bk, bk)
        for c in range(bq // cw):
            cols = slice(c * cw, (c + 1) * cw)
            if diag is None:
                nrow, masked = bk, False
            else:
                nrow = min(max((c + 1) * cw - diag * bk, 0), bk)
                masked = diag * bk + nrow - 1 > c * cw
            if nrow == 0:
                continue
            s = s_sc[0:nrow, cols]
            if masked:
                key = diag * bk + lax.broadcasted_iota(jnp.int32, s.shape, 0)
                qry = c * cw + lax.broadcasted_iota(jnp.int32, s.shape, 1)
                s = jnp.where(key <= qry, s, -jnp.inf)
            m_prev = m_sc[:, cols]
            m_new = jnp.maximum(m_prev, jnp.max(s, axis=0, keepdims=True))
            alpha = jnp.exp2(m_prev - m_new)
            p = jnp.exp2(s - m_new[0:1, :])
            vt = vt_ref[0, :, pl.ds(start, nrow)]
            pv = jnp.dot(vt, p.astype(BF16), preferred_element_type=F32)
            acc_sc[:, cols] = alpha[0:1, :] * acc_sc[:, cols] + pv
            m_sc[:, cols] = m_new

    @pl.when(qi == 0)
    def _():
        qk(q_cur, 0, sa_sc)

    def pair(a):
        qk(q_cur, a + 1, sb_sc)
        softmax_pv(a, sa_sc, None)
        qk(q_cur, a + 2, sa_sc)
        softmax_pv(a + 1, sb_sc, None)

    def four_pairs(t, carry):
        for u in range(4):
            pair(8 * t + 2 * u)
        return carry

    def one_pair(t, carry):
        pair(2 * t)
        return carry

    n_quad = lax.shift_right_logical(qi, 2)
    lax.fori_loop(0, n_quad, four_pairs, 0)
    lax.fori_loop(4 * n_quad, qi, one_pair, 0)

    a = 2 * qi
    qk(q_cur, a + 1, sb_sc, first_col=bk)
    softmax_pv(a, sa_sc, 0)
    qk(q_nxt, 0, sa_sc)
    softmax_pv(a + 1, sb_sc, 1)

    o_t = acc_sc[0:V_HEAD_DIM, :] / acc_sc[V_HEAD_DIM:V_HEAD_DIM + 1, :]
    o_ref[...] = (o_t.T * sg_ref[...].astype(F32)).astype(o_ref.dtype)


def _attn(qt, k, vt, g, bq=1024, cw=256):
    nh, _, s = qt.shape
    bk = bq // 2
    nq = s // bq
    return pl.pallas_call(
        functools.partial(_attn_kernel, bq=bq, bk=bk, cw=cw, nq=nq),
        grid=(nh, nq),
        in_specs=[
            pl.BlockSpec((1, QK_PAD_DIM, s), lambda h, i: (h, 0, 0)),
            pl.BlockSpec((1, s, QK_PAD_DIM), lambda h, i: (h, 0, 0)),
            pl.BlockSpec((1, VT_ROWS, s), lambda h, i: (h, 0, 0)),
            pl.BlockSpec((bq, V_HEAD_DIM), lambda h, i: (i, h)),
        ],
        out_specs=pl.BlockSpec((bq, V_HEAD_DIM), lambda h, i: (i, h)),
        out_shape=jax.ShapeDtypeStruct((s, nh * V_HEAD_DIM), BF16),
        scratch_shapes=[
            pltpu.VMEM((bk, bq), F32),
            pltpu.VMEM((bk, bq), F32),
            pltpu.VMEM((8, bq), F32),
            pltpu.VMEM((VT_ROWS, bq), F32),
        ],
        compiler_params=_params(("arbitrary", "arbitrary")),
        name="attn",
    )(qt, k, vt, g)


def _pool_kernel(u_ref, halo_ref, pg_ref, w_ref, sc_ref, o_ref, *, tm):
    i = pl.program_id(0)
    u = u_ref[...].astype(F32)
    halo = jnp.where(i > 0, halo_ref[...].astype(F32), 0.0)
    ext = jnp.concatenate([halo, u], axis=0)
    t = i * tm + lax.broadcasted_iota(jnp.int32, (tm, 1), 0)
    n_ext = POOL_HALO + tm

    def shifted(a, kk):
        return jnp.concatenate([jnp.zeros((kk, a.shape[1]), F32), a[:a.shape[0] - kk]], axis=0)

    acc = ext
    width = 1
    for g, w in enumerate(POOL_WINDOWS):
        lo = g * POOL_GROUP_DIM
        acc = acc[:, (0 if g == 0 else POOL_GROUP_DIM):]
        while width < w:
            acc = acc + shifted(acc, width)
            width *= 2
        win = acc[POOL_HALO:n_ext, 0:POOL_GROUP_DIM]
        cnt = jnp.minimum(t + 1, w).astype(F32)
        pooled = win / cnt - u[:, lo:lo + POOL_GROUP_DIM]
        mixed = jnp.dot(pooled.astype(BF16), w_ref[g], preferred_element_type=F32)
        mixed = mixed * sc_ref[:, lo:lo + POOL_GROUP_DIM]
        o_ref[:, lo:lo + POOL_GROUP_DIM] = (
            mixed * pg_ref[:, lo:lo + POOL_GROUP_DIM].astype(F32)).astype(o_ref.dtype)


def _pool(g, w_pool, pool_scale, d, tm=512):
    s = g.shape[0]
    p = POOL_WIDTH
    ub = d // p
    hb = tm // POOL_HALO
    return pl.pallas_call(
        functools.partial(_pool_kernel, tm=tm),
        grid=(s // tm,),
        in_specs=[
            pl.BlockSpec((tm, p), lambda i: (i, ub)),
            pl.BlockSpec((POOL_HALO, p), lambda i: (jnp.maximum(i * hb - 1, 0), ub)),
            pl.BlockSpec((tm, p), lambda i: (i, ub + 1)),
            _const_spec(w_pool.shape),
            _const_spec(pool_scale.shape),
        ],
        out_specs=pl.BlockSpec((tm, p), lambda i: (i, 0)),
        out_shape=jax.ShapeDtypeStruct((s, p), BF16),
        compiler_params=_params(("parallel",)),
        name="pool",
    )(g, g, g, w_pool, pool_scale)


def _final_kernel(ya_ref, yp_ref, gm_ref, gp_ref, x_ref, mod_ref, lng_ref, lnb_ref,
                  wmo_ref, wpo_ref, wout_ref, o_ref, *, d):
    y_mla = jnp.dot(ya_ref[...], wmo_ref[...], preferred_element_type=F32)
    y_pool = jnp.dot(yp_ref[...], wpo_ref[...], preferred_element_type=F32)
    merged = gm_ref[...].astype(F32) * y_mla + gp_ref[...].astype(F32) * y_pool
    y = jnp.dot(merged.astype(BF16), wout_ref[...], preferred_element_type=F32)
    gate = mod_ref[:, 2 * d:3 * d]
    r = DEEPNORM_ALPHA * x_ref[...] + (1.0 + gate) * y
    mu = jnp.mean(r, axis=-1, keepdims=True)
    rc = r - mu
    var = jnp.mean(rc * rc, axis=-1, keepdims=True)
    o_ref[...] = rc * lax.rsqrt(var + LN_EPS) * lng_ref[...] + lnb_ref[...]


def _final(ya, yp, g, x2, mod, ln_g, ln_b, wmo, wpo, wout, tm=512):
    s, d = x2.shape
    gb = (d + 2 * POOL_WIDTH) // d
    return pl.pallas_call(
        functools.partial(_final_kernel, d=d),
        grid=(s // tm,),
        in_specs=[
            pl.BlockSpec((tm, ya.shape[1]), lambda i: (i, 0)),
            pl.BlockSpec((tm, yp.shape[1]), lambda i: (i, 0)),
            pl.BlockSpec((tm, d), lambda i: (i, gb)),
            pl.BlockSpec((tm, d), lambda i: (i, gb + 1)),
            pl.BlockSpec((tm, d), lambda i: (i, 0)),
            _const_spec(mod.shape),
            _const_spec(ln_g.shape),
            _const_spec(ln_b.shape),
            _const_spec(wmo.shape),
            _const_spec(wpo.shape),
            _const_spec(wout.shape),
        ],
        out_specs=pl.BlockSpec((tm, d), lambda i: (i, 0)),
        out_shape=jax.ShapeDtypeStruct((s, d), F32),
        compiler_params=_params(("parallel",)),
        name="final",
    )(ya, yp, g, g, x2, mod, ln_g, ln_b, wmo, wpo, wout)


def _swap_halves(w):
    half = QK_ROPE_DIM // 2
    return jnp.concatenate([w[..., half:], w[..., :half]], axis=-1)


def kernel(x, c, positions, w_ada, b_ada, w_in, b_gates, q_norm_g, w_q_b, kv_norm_g, w_kv_b,
           w_mla_o, w_pool_g, pool_scale, w_pool_o, w_out, ln_g, ln_b):
    b, s, d = x.shape
    assert b == 1 and w_ada.shape[0] == 1
    l = 0
    x2 = x.reshape(s, d)

    o_q, o_kv = Q_LORA_RANK, Q_LORA_RANK + KV_LORA_RANK
    o_g = o_kv + QK_ROPE_DIM
    w_kpe = w_in[l][:, o_kv:o_g]
    wlat = jnp.concatenate([w_in[l][:, :o_kv], w_kpe, _swap_halves(w_kpe)], axis=1).astype(BF16)
    wg = w_in[l][:, o_g:].astype(BF16)
    bg = jnp.concatenate([jnp.zeros((d + 2 * POOL_WIDTH,), F32), b_gates[l]]).reshape(1, -1)
    wq3 = w_q_b[l].reshape(Q_LORA_RANK, N_HEADS, QK_HEAD_DIM)
    wq_pe = wq3[..., QK_NOPE_DIM:]
    wq = jnp.concatenate([wq3[..., :QK_NOPE_DIM], wq_pe, _swap_halves(wq_pe)], axis=-1)
    wqt = jnp.transpose(wq, (1, 2, 0)).astype(BF16)
    wkv3 = w_kv_b[l].reshape(KV_LORA_RANK, N_HEADS, QK_NOPE_DIM + V_HEAD_DIM)
    wk = wkv3[..., :QK_NOPE_DIM].reshape(KV_LORA_RANK, N_HEADS * QK_NOPE_DIM).astype(BF16)
    wvt = jnp.transpose(wkv3[..., QK_NOPE_DIM:], (1, 2, 0)).reshape(
        N_HEADS * V_HEAD_DIM, KV_LORA_RANK).astype(BF16)
    inv_freq = ROPE_THETA ** (-jnp.arange(0, QK_ROPE_DIM, 2, dtype=F32) / QK_ROPE_DIM)
    freq = jnp.tile(inv_freq, 4).reshape(1, 128)
    pos_col = positions.reshape(s, 1)

    mod = _mod(c.reshape(d, 1), w_ada[l], b_ada[l].reshape(1, -1))
    h = _ln(x2, mod)
    qt, k, vt = _lat(h, wlat, q_norm_g[l].reshape(1, -1), kv_norm_g[l].reshape(1, -1),
                     wqt, wk, wvt, pos_col, freq)
    g = _gates(h, wg, bg)
    ya = _attn(qt, k, vt, g)
    yp = _pool(g, w_pool_g[l].astype(BF16), pool_scale[l].reshape(1, -1), d)
    out = _final(ya, yp, g, x2, mod, ln_g[l].reshape(1, -1), ln_b[l].reshape(1, -1),
                 w_mla_o[l].astype(BF16), w_pool_o[l].astype(BF16), w_out[l].astype(BF16))
    return out.reshape(b, s, d)
```

```python
import functools
import math

import jax
import jax.numpy as jnp
from jax import lax
from jax.experimental import pallas as pl
from jax.experimental.pallas import tpu as pltpu

N_HEADS = 16
QK_NOPE_DIM = 128
QK_ROPE_DIM = 64
QK_HEAD_DIM = QK_NOPE_DIM + QK_ROPE_DIM
V_HEAD_DIM = 128
Q_LORA_RANK = 512
KV_LORA_RANK = 512
ROPE_THETA = 10000.0
POOL_WINDOWS = (2, 4, 8, 16)
POOL_GROUP_DIM = 256
POOL_WIDTH = len(POOL_WINDOWS) * POOL_GROUP_DIM
RMS_EPS = 1e-6
LN_EPS = 1e-5
DEPTH = 1
DEEPNORM_ALPHA = (2.0 * DEPTH) ** 0.25

QK_PAD_DIM = 256
VT_ROWS = V_HEAD_DIM + 16
POOL_HALO = 16
VMEM_LIMIT = 56 * 1024 * 1024

F32 = jnp.float32
BF16 = jnp.bfloat16


def _const_spec(shape):
    nd = len(shape)
    return pl.BlockSpec(shape, lambda *_: (0,) * nd, pipeline_mode=pl.Buffered(1))


def _params(sem):
    return pltpu.CompilerParams(dimension_semantics=sem, vmem_limit_bytes=VMEM_LIMIT)


def _mod_kernel(c_ref, w_ref, b_ref, o_ref):
    c = c_ref[...]
    sc = c * jax.nn.sigmoid(c)
    o_ref[...] = jnp.sum(sc * w_ref[...], axis=0, keepdims=True) + b_ref[...]


def _mod(c_col, w_ada, b_ada, tn=512):
    d, n = w_ada.shape
    return pl.pallas_call(
        _mod_kernel,
        grid=(n // tn,),
        in_specs=[
            pl.BlockSpec((d, 1), lambda j: (0, 0)),
            pl.BlockSpec((d, tn), lambda j: (0, j)),
            pl.BlockSpec((1, tn), lambda j: (0, j)),
        ],
        out_specs=pl.BlockSpec((1, tn), lambda j: (0, j)),
        out_shape=jax.ShapeDtypeStruct((1, n), F32),
        compiler_params=_params(("arbitrary",)),
        name="mod",
    )(c_col, w_ada, b_ada)


def _ln_kernel(x_ref, mod_ref, h_ref, *, d):
    x = x_ref[...]
    mu = jnp.mean(x, axis=-1, keepdims=True)
    xc = x - mu
    var = jnp.mean(xc * xc, axis=-1, keepdims=True)
    y = xc * lax.rsqrt(var + LN_EPS)
    shift = mod_ref[:, 0:d]
    scale = mod_ref[:, d:2 * d]
    h_ref[...] = (y * (1.0 + scale) + shift).astype(h_ref.dtype)


def _ln(x2, mod, tm=512):
    s, d = x2.shape
    return pl.pallas_call(
        functools.partial(_ln_kernel, d=d),
        grid=(s // tm,),
        in_specs=[
            pl.BlockSpec((tm, d), lambda i: (i, 0)),
            pl.BlockSpec((1, 3 * d), lambda i: (0, 0)),
        ],
        out_specs=pl.BlockSpec((tm, d), lambda i: (i, 0)),
        out_shape=jax.ShapeDtypeStruct((s, d), BF16),
        compiler_params=_params(("parallel",)),
        name="ln",
    )(x2, mod)


def _rms(xf, g):
    return xf * lax.rsqrt(jnp.mean(xf * xf, axis=-1, keepdims=True) + RMS_EPS) * g


def _lat_kernel(h_ref, wlat_ref, qg_ref, kvg_ref, wqt_ref, wk_ref, wvt_ref, pos_ref, freq_ref,
                qt_ref, k_ref, vt_ref, *, q_scale):
    h = h_ref[...]
    lat = jnp.dot(h, wlat_ref[...], preferred_element_type=F32)
    qn = _rms(lat[:, 0:Q_LORA_RANK], qg_ref[...])
    cn = _rms(lat[:, Q_LORA_RANK:Q_LORA_RANK + KV_LORA_RANK], kvg_ref[...])
    kpe = lat[:, Q_LORA_RANK + KV_LORA_RANK:]
    qn_t = qn.T.astype(BF16)
    cn_t = cn.T.astype(BF16)
    cn = cn.astype(BF16)

    ang = pos_ref[...].astype(F32) * freq_ref[...]
    lane = lax.broadcasted_iota(jnp.int32, ang.shape, 1)
    cs = jnp.where(lane < 64, jnp.cos(ang),
                   jnp.where(lane < 96, -jnp.sin(ang), jnp.sin(ang)))
    cs_t = cs.T

    r = kpe * cs
    k_rot = jnp.where(lane < 64, r + pltpu.roll(r, 64, 1), 0.0).astype(BF16)
    zeros_t = jnp.zeros((QK_PAD_DIM - QK_HEAD_DIM, qn_t.shape[1]), BF16)
    ones_t = jnp.ones((VT_ROWS - V_HEAD_DIM, qn_t.shape[1]), BF16)
    for hd in range(N_HEADS):
        qh = jnp.dot(wqt_ref[hd], qn_t, preferred_element_type=F32)
        rq = qh[128:256, :] * cs_t
        qt_ref[hd, 0:128, :] = (qh[0:128, :] * q_scale).astype(BF16)
        qt_ref[hd, 128:192, :] = ((rq[0:64, :] + rq[64:128, :]) * q_scale).astype(BF16)
        qt_ref[hd, 192:256, :] = zeros_t
    for hp in range(N_HEADS // 2):
        kk = jnp.dot(cn, wk_ref[:, hp * 256:(hp + 1) * 256], preferred_element_type=F32)
        vv = jnp.dot(wvt_ref[hp * 256:(hp + 1) * 256, :], cn_t, preferred_element_type=F32)
        for e in range(2):
            hd = 2 * hp + e
            k_ref[hd, :, 0:128] = kk[:, e * 128:(e + 1) * 128].astype(BF16)
            k_ref[hd, :, 128:256] = k_rot
            vt_ref[hd, 0:V_HEAD_DIM, :] = vv[e * 128:(e + 1) * 128, :].astype(BF16)
            vt_ref[hd, V_HEAD_DIM:VT_ROWS, :] = ones_t


def _lat(h, wlat, qg, kvg, wqt, wk, wvt, pos_col, freq, tm=256):
    s, d = h.shape
    q_scale = (QK_HEAD_DIM ** -0.5) * math.log2(math.e)
    return pl.pallas_call(
        functools.partial(_lat_kernel, q_scale=q_scale),
        grid=(s // tm,),
        in_specs=[
            pl.BlockSpec((tm, d), lambda i: (i, 0)),
            _const_spec(wlat.shape),
            _const_spec(qg.shape),
            _const_spec(kvg.shape),
            _const_spec(wqt.shape),
            _const_spec(wk.shape),
            _const_spec(wvt.shape),
            pl.BlockSpec((tm, 1), lambda i: (i, 0)),
            _const_spec(freq.shape),
        ],
        out_specs=[
            pl.BlockSpec((N_HEADS, QK_PAD_DIM, tm), lambda i: (0, 0, i)),
            pl.BlockSpec((N_HEADS, tm, QK_PAD_DIM), lambda i: (0, i, 0)),
            pl.BlockSpec((N_HEADS, VT_ROWS, tm), lambda i: (0, 0, i)),
        ],
        out_shape=[
            jax.ShapeDtypeStruct((N_HEADS, QK_PAD_DIM, s), BF16),
            jax.ShapeDtypeStruct((N_HEADS, s, QK_PAD_DIM), BF16),
            jax.ShapeDtypeStruct((N_HEADS, VT_ROWS, s), BF16),
        ],
        compiler_params=_params(("parallel",)),
        name="lat",
    )(h, wlat, qg, kvg, wqt, wk, wvt, pos_col, freq)


def _gates_kernel(h_ref, w_ref, b_ref, o_ref, *, tn, cn, d, pool_w):
    j = pl.program_id(1)
    h = h_ref[...]
    for c in range(tn // cn):
        cols = slice(c * cn, (c + 1) * cn)
        col = j * tn + c * cn
        is_ident = jnp.logical_and(col >= d, col < d + pool_w)
        is_sigm = col >= d + 2 * pool_w
        z = jnp.dot(h, w_ref[:, cols], preferred_element_type=F32)
        sg = jax.nn.sigmoid(z + b_ref[:, cols])
        o_ref[:, cols] = jnp.where(is_ident, z, jnp.where(is_sigm, sg, z * sg)).astype(o_ref.dtype)


def _gates(h, wg, bg, tm=1024, tn=2048, cn=512):
    s, d = h.shape
    n = wg.shape[1]
    return pl.pallas_call(
        functools.partial(_gates_kernel, tn=tn, cn=cn, d=d, pool_w=POOL_WIDTH),
        grid=(s // tm, n // tn),
        in_specs=[
            pl.BlockSpec((tm, d), lambda i, j: (i, 0)),
            pl.BlockSpec((d, tn), lambda i, j: (0, j)),
            pl.BlockSpec((1, tn), lambda i, j: (0, j)),
        ],
        out_specs=pl.BlockSpec((tm, tn), lambda i, j: (i, j)),
        out_shape=jax.ShapeDtypeStruct((s, n), BF16),
        compiler_params=_params(("parallel", "arbitrary")),
        name="gates",
    )(h, wg, bg)


def _attn_kernel(qt_ref, k_ref, vt_ref, sg_ref, o_ref, sa_sc, sb_sc, m_sc, acc_sc,
                 *, bq, bk, cw, nq):
    qi = pl.program_id(1)
    q_cur = pl.multiple_of(qi * bq, bq)
    q_nxt = pl.multiple_of(jnp.minimum(qi + 1, nq - 1) * bq, bq)
    m_sc[...] = jnp.full(m_sc.shape, -jnp.inf, F32)
    acc_sc[...] = jnp.zeros(acc_sc.shape, F32)

    def qk(q_start, j, s_sc, first_col=0):
        start = pl.multiple_of(j * bk, bk)
        q_t = qt_ref[0, :, pl.ds(pl.multiple_of(q_start + first_col, cw), bq - first_col)]
        s_sc[:, first_col:bq] = jnp.dot(k_ref[0, pl.ds(start, bk), :], q_t,
                                        preferred_element_type=F32)

    def softmax_pv(j, s_sc, diag):
        start = pl.multiple_of(j * bk, bk)
        for c in range(bq // cw):
            cols = slice(c * cw, (c + 1) * cw)
            if diag is None:
                nrow, masked = bk, False
            else:
                nrow = min(max((c + 1) * cw - diag * bk, 0), bk)
                masked = diag * bk + nrow - 1 > c * cw
            if nrow == 0:
                continue
            s = s_sc[0:nrow, cols]
            if masked:
                key = diag * bk + lax.broadcasted_iota(jnp.int32, s.shape, 0)
                qry = c * cw + lax.broadcasted_iota(jnp.int32, s.shape, 1)
                s = jnp.where(key <= qry, s, -jnp.inf)
            m_prev = m_sc[:, cols]
            m_new = jnp.maximum(m_prev, jnp.max(s, axis=0, keepdims=True))
            alpha = jnp.exp2(m_prev - m_new)
            p = jnp.exp2(s - m_new[0:1, :])
            vt = vt_ref[0, :, pl.ds(start, nrow)]
            pv = jnp.dot(vt, p.astype(BF16), preferred_element_type=F32)
            acc_sc[:, cols] = alpha[0:1, :] * acc_sc[:, cols] + pv
            m_sc[:, cols] = m_new

    @pl.when(qi == 0)
    def _():
        qk(q_cur, 0, sa_sc)

    def pair(a):
        qk(q_cur, a + 1, sb_sc)
        softmax_pv(a, sa_sc, None)
        qk(q_cur, a + 2, sa_sc)
        softmax_pv(a + 1, sb_sc, None)

    def four_pairs(t, carry):
        for u in range(4):
            pair(8 * t + 2 * u)
        return carry

    def one_pair(t, carry):
        pair(2 * t)
        return carry

    n_quad = lax.shift_right_logical(qi, 2)
    lax.fori_loop(0, n_quad, four_pairs, 0)
    lax.fori_loop(4 * n_quad, qi, one_pair, 0)

    a = 2 * qi
    qk(q_cur, a + 1, sb_sc, first_col=bk)
    softmax_pv(a, sa_sc, 0)
    qk(q_nxt, 0, sa_sc)
    softmax_pv(a + 1, sb_sc, 1)

    o_t = acc_sc[0:V_HEAD_DIM, :] / acc_sc[V_HEAD_DIM:V_HEAD_DIM + 1, :]
    o_ref[...] = (o_t.T * sg_ref[...].astype(F32)).astype(o_ref.dtype)


def _attn(qt, k, vt, g, bq=1024, cw=256):
    nh, _, s = qt.shape
    bk = bq // 2
    nq = s // bq
    return pl.pallas_call(
        functools.partial(_attn_kernel, bq=bq, bk=bk, cw=cw, nq=nq),
        grid=(nh, nq),
        in_specs=[
            pl.BlockSpec((1, QK_PAD_DIM, s), lambda h, i: (h, 0, 0)),
            pl.BlockSpec((1, s, QK_PAD_DIM), lambda h, i: (h, 0, 0)),
            pl.BlockSpec((1, VT_ROWS, s), lambda h, i: (h, 0, 0)),
            pl.BlockSpec((bq, V_HEAD_DIM), lambda h, i: (i, h)),
        ],
        out_specs=pl.BlockSpec((bq, V_HEAD_DIM), lambda h, i: (i, h)),
        out_shape=jax.ShapeDtypeStruct((s, nh * V_HEAD_DIM), BF16),
        scratch_shapes=[
            pltpu.VMEM((bk, bq), F32),
            pltpu.VMEM((bk, bq), F32),
            pltpu.VMEM((8, bq), F32),
            pltpu.VMEM((VT_ROWS, bq), F32),
        ],
        compiler_params=_params(("arbitrary", "arbitrary")),
        name="attn",
    )(qt, k, vt, g)


def _pool_kernel(u_ref, halo_ref, pg_ref, w_ref, sc_ref, o_ref, *, tm):
    i = pl.program_id(0)
    u = u_ref[...].astype(F32)
    halo = jnp.where(i > 0, halo_ref[...].astype(F32), 0.0)
    ext = jnp.concatenate([halo, u], axis=0)
    t = i * tm + lax.broadcasted_iota(jnp.int32, (tm, 1), 0)
    n_ext = POOL_HALO + tm

    def shifted(a, kk):
        return jnp.concatenate([jnp.zeros((kk, a.shape[1]), F32), a[:a.shape[0] - kk]], axis=0)

    acc = ext
    width = 1
    for g, w in enumerate(POOL_WINDOWS):
        lo = g * POOL_GROUP_DIM
        acc = acc[:, (0 if g == 0 else POOL_GROUP_DIM):]
        while width < w:
            acc = acc + shifted(acc, width)
            width *= 2
        win = acc[POOL_HALO:n_ext, 0:POOL_GROUP_DIM]
        cnt = jnp.minimum(t + 1, w).astype(F32)
        pooled = win / cnt - u[:, lo:lo + POOL_GROUP_DIM]
        mixed = jnp.dot(pooled.astype(BF16), w_ref[g], preferred_element_type=F32)
        mixed = mixed * sc_ref[:, lo:lo + POOL_GROUP_DIM]
        o_ref[:, lo:lo + POOL_GROUP_DIM] = (
            mixed * pg_ref[:, lo:lo + POOL_GROUP_DIM].astype(F32)).astype(o_ref.dtype)


def _pool(g, w_pool, pool_scale, d, tm=512):
    s = g.shape[0]
    p = POOL_WIDTH
    ub = d // p
    hb = tm // POOL_HALO
    return pl.pallas_call(
        functools.partial(_pool_kernel, tm=tm),
        grid=(s // tm,),
        in_specs=[
            pl.BlockSpec((tm, p), lambda i: (i, ub)),
            pl.BlockSpec((POOL_HALO, p), lambda i: (jnp.maximum(i * hb - 1, 0), ub)),
            pl.BlockSpec((tm, p), lambda i: (i, ub + 1)),
            _const_spec(w_pool.shape),
            _const_spec(pool_scale.shape),
        ],
        out_specs=pl.BlockSpec((tm, p), lambda i: (i, 0)),
        out_shape=jax.ShapeDtypeStruct((s, p), BF16),
        compiler_params=_params(("parallel",)),
        name="pool",
    )(g, g, g, w_pool, pool_scale)


def _final_kernel(ya_ref, yp_ref, gm_ref, gp_ref, x_ref, mod_ref, lng_ref, lnb_ref,
                  wmo_ref, wpo_ref, wout_ref, o_ref, *, d):
    y_mla = jnp.dot(ya_ref[...], wmo_ref[...], preferred_element_type=F32)
    y_pool = jnp.dot(yp_ref[...], wpo_ref[...], preferred_element_type=F32)
    merged = gm_ref[...].astype(F32) * y_mla + gp_ref[...].astype(F32) * y_pool
    y = jnp.dot(merged.astype(BF16), wout_ref[...], preferred_element_type=F32)
    gate = mod_ref[:, 2 * d:3 * d]
    r = DEEPNORM_ALPHA * x_ref[...] + (1.0 + gate) * y
    mu = jnp.mean(r, axis=-1, keepdims=True)
    rc = r - mu
    var = jnp.mean(rc * rc, axis=-1, keepdims=True)
    o_ref[...] = rc * lax.rsqrt(var + LN_EPS) * lng_ref[...] + lnb_ref[...]


def _final(ya, yp, g, x2, mod, ln_g, ln_b, wmo, wpo, wout, tm=512):
    s, d = x2.shape
    gb = (d + 2 * POOL_WIDTH) // d
    return pl.pallas_call(
        functools.partial(_final_kernel, d=d),
        grid=(s // tm,),
        in_specs=[
            pl.BlockSpec((tm, ya.shape[1]), lambda i: (i, 0)),
            pl.BlockSpec((tm, yp.shape[1]), lambda i: (i, 0)),
            pl.BlockSpec((tm, d), lambda i: (i, gb)),
            pl.BlockSpec((tm, d), lambda i: (i, gb + 1)),
            pl.BlockSpec((tm, d), lambda i: (i, 0)),
            _const_spec(mod.shape),
            _const_spec(ln_g.shape),
            _const_spec(ln_b.shape),
            _const_spec(wmo.shape),
            _const_spec(wpo.shape),
            _const_spec(wout.shape),
        ],
        out_specs=pl.BlockSpec((tm, d), lambda i: (i, 0)),
        out_shape=jax.ShapeDtypeStruct((s, d), F32),
        compiler_params=_params(("parallel",)),
        name="final",
    )(ya, yp, g, g, x2, mod, ln_g, ln_b, wmo, wpo, wout)


def _swap_halves(w):
    half = QK_ROPE_DIM // 2
    return jnp.concatenate([w[..., half:], w[..., :half]], axis=-1)


def _prep_kernel(w_ref, wlat_ref, wg_ref):
    o_kv = Q_LORA_RANK + KV_LORA_RANK
    o_g = o_kv + QK_ROPE_DIM
    w = w_ref[...]
    wlat_ref[...] = jnp.concatenate([w[:, :o_g], _swap_halves(w[:, o_kv:o_g])], axis=1).astype(BF16)
    wg_ref[...] = w[:, o_g:].astype(BF16)


def _prep_w_in(w, tr=256):
    d, n = w.shape
    n_lat = Q_LORA_RANK + KV_LORA_RANK + 2 * QK_ROPE_DIM
    n_g = n - n_lat + QK_ROPE_DIM
    return pl.pallas_call(
        _prep_kernel,
        grid=(d // tr,),
        in_specs=[pl.BlockSpec((tr, n), lambda i: (i, 0))],
        out_specs=[pl.BlockSpec((tr, n_lat), lambda i: (i, 0)),
                   pl.BlockSpec((tr, n_g), lambda i: (i, 0))],
        out_shape=[jax.ShapeDtypeStruct((d, n_lat), BF16), jax.ShapeDtypeStruct((d, n_g), BF16)],
        compiler_params=_params(("parallel",)),
        name="prep",
    )(w)


def kernel(x, c, positions, w_ada, b_ada, w_in, b_gates, q_norm_g, w_q_b, kv_norm_g, w_kv_b,
           w_mla_o, w_pool_g, pool_scale, w_pool_o, w_out, ln_g, ln_b):
    b, s, d = x.shape
    assert b == 1 and w_ada.shape[0] == 1
    l = 0
    x2 = x.reshape(s, d)

    wlat, wg = _prep_w_in(w_in[l])
    bg = jnp.concatenate([jnp.zeros((d + 2 * POOL_WIDTH,), F32), b_gates[l]]).reshape(1, -1)
    wq3 = w_q_b[l].reshape(Q_LORA_RANK, N_HEADS, QK_HEAD_DIM)
    wq_pe = wq3[..., QK_NOPE_DIM:]
    wq = jnp.concatenate([wq3[..., :QK_NOPE_DIM], wq_pe, _swap_halves(wq_pe)], axis=-1)
    wqt = jnp.transpose(wq, (1, 2, 0)).astype(BF16)
    wkv3 = w_kv_b[l].reshape(KV_LORA_RANK, N_HEADS, QK_NOPE_DIM + V_HEAD_DIM)
    wk = wkv3[..., :QK_NOPE_DIM].reshape(KV_LORA_RANK, N_HEADS * QK_NOPE_DIM).astype(BF16)
    wvt = jnp.transpose(wkv3[..., QK_NOPE_DIM:], (1, 2, 0)).reshape(
        N_HEADS * V_HEAD_DIM, KV_LORA_RANK).astype(BF16)
    inv_freq = ROPE_THETA ** (-jnp.arange(0, QK_ROPE_DIM, 2, dtype=F32) / QK_ROPE_DIM)
    freq = jnp.tile(inv_freq, 4).reshape(1, 128)
    pos_col = positions.reshape(s, 1)

    mod = _mod(c.reshape(d, 1), w_ada[l], b_ada[l].reshape(1, -1))
    h = _ln(x2, mod)
    qt, k, vt = _lat(h, wlat, q_norm_g[l].reshape(1, -1), kv_norm_g[l].reshape(1, -1),
                     wqt, wk, wvt, pos_col, freq)
    g = _gates(h, wg, bg)
    ya = _attn(qt, k, vt, g)
    yp = _pool(g, w_pool_g[l].astype(BF16), pool_scale[l].reshape(1, -1), d)
    out = _final(ya, yp, g, x2, mod, ln_g[l].reshape(1, -1), ln_b[l].reshape(1, -1),
                 w_mla_o[l].astype(BF16), w_pool_o[l].astype(BF16), w_out[l].astype(BF16))
    return out.reshape(b, s, d)
```

```python
import functools
import math

import jax
import jax.numpy as jnp
from jax import lax
from jax.experimental import pallas as pl
from jax.experimental.pallas import tpu as pltpu

N_HEADS = 16
QK_NOPE_DIM = 128
QK_ROPE_DIM = 64
QK_HEAD_DIM = QK_NOPE_DIM + QK_ROPE_DIM
V_HEAD_DIM = 128
Q_LORA_RANK = 512
KV_LORA_RANK = 512
ROPE_THETA = 10000.0
POOL_WINDOWS = (2, 4, 8, 16)
POOL_GROUP_DIM = 256
POOL_WIDTH = len(POOL_WINDOWS) * POOL_GROUP_DIM
RMS_EPS = 1e-6
LN_EPS = 1e-5
DEPTH = 1
DEEPNORM_ALPHA = (2.0 * DEPTH) ** 0.25

QK_PAD_DIM = 256
VT_ROWS = V_HEAD_DIM + 16
POOL_HALO = 16
VMEM_LIMIT = 56 * 1024 * 1024

F32 = jnp.float32
BF16 = jnp.bfloat16
_NT_DIMS = (((1,), (1,)), ((), ()))


def _const_spec(shape):
    nd = len(shape)
    return pl.BlockSpec(shape, lambda *_: (0,) * nd, pipeline_mode=pl.Buffered(1))


def _params(sem):
    return pltpu.CompilerParams(dimension_semantics=sem, vmem_limit_bytes=VMEM_LIMIT)


def _mod_kernel(c_ref, w_ref, b_ref, o_ref):
    c = c_ref[...]
    sc = c * jax.nn.sigmoid(c)
    o_ref[...] = jnp.sum(sc * w_ref[...], axis=0, keepdims=True) + b_ref[...]


def _mod(c_col, w_ada, b_ada, tn=512):
    d, n = w_ada.shape
    return pl.pallas_call(
        _mod_kernel,
        grid=(n // tn,),
        in_specs=[
            pl.BlockSpec((d, 1), lambda j: (0, 0)),
            pl.BlockSpec((d, tn), lambda j: (0, j)),
            pl.BlockSpec((1, tn), lambda j: (0, j)),
        ],
        out_specs=pl.BlockSpec((1, tn), lambda j: (0, j)),
        out_shape=jax.ShapeDtypeStruct((1, n), F32),
        compiler_params=_params(("arbitrary",)),
        name="mod",
    )(c_col, w_ada, b_ada)


def _ln_kernel(x_ref, mod_ref, h_ref, *, d):
    x = x_ref[...]
    mu = jnp.mean(x, axis=-1, keepdims=True)
    xc = x - mu
    var = jnp.mean(xc * xc, axis=-1, keepdims=True)
    y = xc * lax.rsqrt(var + LN_EPS)
    shift = mod_ref[:, 0:d]
    scale = mod_ref[:, d:2 * d]
    h_ref[...] = (y * (1.0 + scale) + shift).astype(h_ref.dtype)


def _ln(x2, mod, tm=512):
    s, d = x2.shape
    return pl.pallas_call(
        functools.partial(_ln_kernel, d=d),
        grid=(s // tm,),
        in_specs=[
            pl.BlockSpec((tm, d), lambda i: (i, 0)),
            pl.BlockSpec((1, 3 * d), lambda i: (0, 0)),
        ],
        out_specs=pl.BlockSpec((tm, d), lambda i: (i, 0)),
        out_shape=jax.ShapeDtypeStruct((s, d), BF16),
        compiler_params=_params(("parallel",)),
        name="ln",
    )(x2, mod)


def _rms(xf, g):
    return xf * lax.rsqrt(jnp.mean(xf * xf, axis=-1, keepdims=True) + RMS_EPS) * g


def _lat_kernel(h_ref, wlat_ref, qg_ref, kvg_ref, wqt_ref, wk_ref, wvt_ref, pos_ref, freq_ref,
                qt_ref, k_ref, vt_ref, *, q_scale):
    h = h_ref[...]
    lat = lax.dot_general(h, wlat_ref[...], _NT_DIMS, preferred_element_type=F32)
    qn = _rms(lat[:, 0:Q_LORA_RANK], qg_ref[...])
    cn = _rms(lat[:, Q_LORA_RANK:Q_LORA_RANK + KV_LORA_RANK], kvg_ref[...])
    kpe = lat[:, Q_LORA_RANK + KV_LORA_RANK:]
    qn_t = qn.T.astype(BF16)
    cn_t = cn.T.astype(BF16)
    cn = cn.astype(BF16)

    ang = pos_ref[...].astype(F32) * freq_ref[...]
    lane = lax.broadcasted_iota(jnp.int32, ang.shape, 1)
    cs = jnp.where(lane < 64, jnp.cos(ang),
                   jnp.where(lane < 96, -jnp.sin(ang), jnp.sin(ang)))
    cs_t = cs.T

    r = kpe * cs
    k_rot = jnp.where(lane < 64, r + pltpu.roll(r, 64, 1), 0.0).astype(BF16)
    zeros_t = jnp.zeros((QK_PAD_DIM - QK_HEAD_DIM, qn_t.shape[1]), BF16)
    ones_t = jnp.ones((VT_ROWS - V_HEAD_DIM, qn_t.shape[1]), BF16)
    for hd in range(N_HEADS):
        qh = jnp.dot(wqt_ref[hd], qn_t, preferred_element_type=F32)
        rq = qh[128:256, :] * cs_t
        qt_ref[hd, 0:128, :] = (qh[0:128, :] * q_scale).astype(BF16)
        qt_ref[hd, 128:192, :] = ((rq[0:64, :] + rq[64:128, :]) * q_scale).astype(BF16)
        qt_ref[hd, 192:256, :] = zeros_t
    for hp in range(N_HEADS // 2):
        kk = jnp.dot(cn, wk_ref[:, hp * 256:(hp + 1) * 256], preferred_element_type=F32)
        vv = jnp.dot(wvt_ref[hp * 256:(hp + 1) * 256, :], cn_t, preferred_element_type=F32)
        for e in range(2):
            hd = 2 * hp + e
            k_ref[hd, :, 0:128] = kk[:, e * 128:(e + 1) * 128].astype(BF16)
            k_ref[hd, :, 128:256] = k_rot
            vt_ref[hd, 0:V_HEAD_DIM, :] = vv[e * 128:(e + 1) * 128, :].astype(BF16)
            vt_ref[hd, V_HEAD_DIM:VT_ROWS, :] = ones_t


def _lat(h, wlat, qg, kvg, wqt, wk, wvt, pos_col, freq, tm=256):
    s, d = h.shape
    q_scale = (QK_HEAD_DIM ** -0.5) * math.log2(math.e)
    return pl.pallas_call(
        functools.partial(_lat_kernel, q_scale=q_scale),
        grid=(s // tm,),
        in_specs=[
            pl.BlockSpec((tm, d), lambda i: (i, 0)),
            _const_spec(wlat.shape),
            _const_spec(qg.shape),
            _const_spec(kvg.shape),
            _const_spec(wqt.shape),
            _const_spec(wk.shape),
            _const_spec(wvt.shape),
            pl.BlockSpec((tm, 1), lambda i: (i, 0)),
            _const_spec(freq.shape),
        ],
        out_specs=[
            pl.BlockSpec((N_HEADS, QK_PAD_DIM, tm), lambda i: (0, 0, i)),
            pl.BlockSpec((N_HEADS, tm, QK_PAD_DIM), lambda i: (0, i, 0)),
            pl.BlockSpec((N_HEADS, VT_ROWS, tm), lambda i: (0, 0, i)),
        ],
        out_shape=[
            jax.ShapeDtypeStruct((N_HEADS, QK_PAD_DIM, s), BF16),
            jax.ShapeDtypeStruct((N_HEADS, s, QK_PAD_DIM), BF16),
            jax.ShapeDtypeStruct((N_HEADS, VT_ROWS, s), BF16),
        ],
        compiler_params=_params(("parallel",)),
        name="lat",
    )(h, wlat, qg, kvg, wqt, wk, wvt, pos_col, freq)


def _gates_kernel(h_ref, w_ref, b_ref, o_ref, *, tn, cn, d, pool_w):
    j = pl.program_id(1)
    h = h_ref[...]
    for c in range(tn // cn):
        cols = slice(c * cn, (c + 1) * cn)
        col = j * tn + c * cn
        is_ident = jnp.logical_and(col >= d, col < d + pool_w)
        is_sigm = col >= d + 2 * pool_w
        z = lax.dot_general(h, w_ref[cols, :], _NT_DIMS, preferred_element_type=F32)
        sg = jax.nn.sigmoid(z + b_ref[:, cols])
        o_ref[:, cols] = jnp.where(is_ident, z, jnp.where(is_sigm, sg, z * sg)).astype(o_ref.dtype)


def _gates(h, wg_t, bg, tm=1024, tn=2048, cn=512):
    s, d = h.shape
    n = wg_t.shape[0]
    return pl.pallas_call(
        functools.partial(_gates_kernel, tn=tn, cn=cn, d=d, pool_w=POOL_WIDTH),
        grid=(s // tm, n // tn),
        in_specs=[
            pl.BlockSpec((tm, d), lambda i, j: (i, 0)),
            pl.BlockSpec((tn, d), lambda i, j: (j, 0)),
            pl.BlockSpec((1, tn), lambda i, j: (0, j)),
        ],
        out_specs=pl.BlockSpec((tm, tn), lambda i, j: (i, j)),
        out_shape=jax.ShapeDtypeStruct((s, n), BF16),
        compiler_params=_params(("parallel", "arbitrary")),
        name="gates",
    )(h, wg_t, bg)


def _attn_kernel(qt_ref, k_ref, vt_ref, sg_ref, o_ref, sa_sc, sb_sc, m_sc, acc_sc,
                 *, bq, bk, cw, nq):
    qi = pl.program_id(1)
    q_cur = pl.multiple_of(qi * bq, bq)
    q_nxt = pl.multiple_of(jnp.minimum(qi + 1, nq - 1) * bq, bq)
    m_sc[...] = jnp.full(m_sc.shape, -jnp.inf, F32)
    acc_sc[...] = jnp.zeros(acc_sc.shape, F32)

    def qk(q_start, j, s_sc, first_col=0):
        start = pl.multiple_of(j * bk, bk)
        q_t = qt_ref[0, :, pl.ds(pl.multiple_of(q_start + first_col, cw), bq - first_col)]
        s_sc[:, first_col:bq] = jnp.dot(k_ref[0, pl.ds(start, bk), :], q_t,
                                        preferred_element_type=F32)

    def softmax_pv(j, s_sc, diag):
        start = pl.multiple_of(j * bk, bk)
        for c in range(bq // cw):
            cols = slice(c * cw, (c + 1) * cw)
            if diag is None:
                nrow, masked = bk, False
            else:
                nrow = min(max((c + 1) * cw - diag * bk, 0), bk)
                masked = diag * bk + nrow - 1 > c * cw
            if nrow == 0:
                continue
            s = s_sc[0:nrow, cols]
            if masked:
                key = diag * bk + lax.broadcasted_iota(jnp.int32, s.shape, 0)
                qry = c * cw + lax.broadcasted_iota(jnp.int32, s.shape, 1)
                s = jnp.where(key <= qry, s, -jnp.inf)
            m_prev = m_sc[:, cols]
            m_new = jnp.maximum(m_prev, jnp.max(s, axis=0, keepdims=True))
            alpha = jnp.exp2(m_prev - m_new)
            p = jnp.exp2(s - m_new[0:1, :])
            vt = vt_ref[0, :, pl.ds(start, nrow)]
            pv = jnp.dot(vt, p.astype(BF16), preferred_element_type=F32)
            acc_sc[:, cols] = alpha[0:1, :] * acc_sc[:, cols] + pv
            m_sc[:, cols] = m_new

    @pl.when(qi == 0)
    def _():
        qk(q_cur, 0, sa_sc)

    def pair(a):
        qk(q_cur, a + 1, sb_sc)
        softmax_pv(a, sa_sc, None)
        qk(q_cur, a + 2, sa_sc)
        softmax_pv(a + 1, sb_sc, None)

    def four_pairs(t, carry):
        for u in range(4):
            pair(8 * t + 2 * u)
        return carry

    def one_pair(t, carry):
        pair(2 * t)
        return carry

    n_quad = lax.shift_right_logical(qi, 2)
    lax.fori_loop(0, n_quad, four_pairs, 0)
    lax.fori_loop(4 * n_quad, qi, one_pair, 0)

    a = 2 * qi
    qk(q_cur, a + 1, sb_sc, first_col=bk)
    softmax_pv(a, sa_sc, 0)
    qk(q_nxt, 0, sa_sc)
    softmax_pv(a + 1, sb_sc, 1)

    o_t = acc_sc[0:V_HEAD_DIM, :] / acc_sc[V_HEAD_DIM:V_HEAD_DIM + 1, :]
    o_ref[...] = (o_t.T * sg_ref[...].astype(F32)).astype(o_ref.dtype)


def _attn(qt, k, vt, g, bq=1024, cw=256):
    nh, _, s = qt.shape
    bk = bq // 2
    nq = s // bq
    return pl.pallas_call(
        functools.partial(_attn_kernel, bq=bq, bk=bk, cw=cw, nq=nq),
        grid=(nh, nq),
        in_specs=[
            pl.BlockSpec((1, QK_PAD_DIM, s), lambda h, i: (h, 0, 0)),
            pl.BlockSpec((1, s, QK_PAD_DIM), lambda h, i: (h, 0, 0)),
            pl.BlockSpec((1, VT_ROWS, s), lambda h, i: (h, 0, 0)),
            pl.BlockSpec((bq, V_HEAD_DIM), lambda h, i: (i, h)),
        ],
        out_specs=pl.BlockSpec((bq, V_HEAD_DIM), lambda h, i: (i, h)),
        out_shape=jax.ShapeDtypeStruct((s, nh * V_HEAD_DIM), BF16),
        scratch_shapes=[
            pltpu.VMEM((bk, bq), F32),
            pltpu.VMEM((bk, bq), F32),
            pltpu.VMEM((8, bq), F32),
            pltpu.VMEM((VT_ROWS, bq), F32),
        ],
        compiler_params=_params(("arbitrary", "arbitrary")),
        name="attn",
    )(qt, k, vt, g)


def _pool_kernel(u_ref, halo_ref, pg_ref, w_ref, sc_ref, o_ref, *, tm):
    i = pl.program_id(0)
    u = u_ref[...].astype(F32)
    halo = jnp.where(i > 0, halo_ref[...].astype(F32), 0.0)
    ext = jnp.concatenate([halo, u], axis=0)
    t = i * tm + lax.broadcasted_iota(jnp.int32, (tm, 1), 0)
    n_ext = POOL_HALO + tm

    def shifted(a, kk):
        return jnp.concatenate([jnp.zeros((kk, a.shape[1]), F32), a[:a.shape[0] - kk]], axis=0)

    acc = ext
    width = 1
    for g, w in enumerate(POOL_WINDOWS):
        lo = g * POOL_GROUP_DIM
        acc = acc[:, (0 if g == 0 else POOL_GROUP_DIM):]
        while width < w:
            acc = acc + shifted(acc, width)
            width *= 2
        win = acc[POOL_HALO:n_ext, 0:POOL_GROUP_DIM]
        cnt = jnp.minimum(t + 1, w).astype(F32)
        pooled = win / cnt - u[:, lo:lo + POOL_GROUP_DIM]
        mixed = jnp.dot(pooled.astype(BF16), w_ref[g], preferred_element_type=F32)
        mixed = mixed * sc_ref[:, lo:lo + POOL_GROUP_DIM]
        o_ref[:, lo:lo + POOL_GROUP_DIM] = (
            mixed * pg_ref[:, lo:lo + POOL_GROUP_DIM].astype(F32)).astype(o_ref.dtype)


def _pool(g, w_pool, pool_scale, d, tm=512):
    s = g.shape[0]
    p = POOL_WIDTH
    ub = d // p
    hb = tm // POOL_HALO
    return pl.pallas_call(
        functools.partial(_pool_kernel, tm=tm),
        grid=(s // tm,),
        in_specs=[
            pl.BlockSpec((tm, p), lambda i: (i, ub)),
            pl.BlockSpec((POOL_HALO, p), lambda i: (jnp.maximum(i * hb - 1, 0), ub)),
            pl.BlockSpec((tm, p), lambda i: (i, ub + 1)),
            _const_spec(w_pool.shape),
            _const_spec(pool_scale.shape),
        ],
        out_specs=pl.BlockSpec((tm, p), lambda i: (i, 0)),
        out_shape=jax.ShapeDtypeStruct((s, p), BF16),
        compiler_params=_params(("parallel",)),
        name="pool",
    )(g, g, g, w_pool, pool_scale)


def _final_kernel(ya_ref, yp_ref, gm_ref, gp_ref, x_ref, mod_ref, lng_ref, lnb_ref,
                  wmo_ref, wpo_ref, wout_ref, o_ref, *, d):
    y_mla = jnp.dot(ya_ref[...], wmo_ref[...], preferred_element_type=F32)
    y_pool = jnp.dot(yp_ref[...], wpo_ref[...], preferred_element_type=F32)
    merged = gm_ref[...].astype(F32) * y_mla + gp_ref[...].astype(F32) * y_pool
    y = jnp.dot(merged.astype(BF16), wout_ref[...], preferred_element_type=F32)
    gate = mod_ref[:, 2 * d:3 * d]
    r = DEEPNORM_ALPHA * x_ref[...] + (1.0 + gate) * y
    mu = jnp.mean(r, axis=-1, keepdims=True)
    rc = r - mu
    var = jnp.mean(rc * rc, axis=-1, keepdims=True)
    o_ref[...] = rc * lax.rsqrt(var + LN_EPS) * lng_ref[...] + lnb_ref[...]


def _final(ya, yp, g, x2, mod, ln_g, ln_b, wmo, wpo, wout, tm=512):
    s, d = x2.shape
    gb = (d + 2 * POOL_WIDTH) // d
    return pl.pallas_call(
        functools.partial(_final_kernel, d=d),
        grid=(s // tm,),
        in_specs=[
            pl.BlockSpec((tm, ya.shape[1]), lambda i: (i, 0)),
            pl.BlockSpec((tm, yp.shape[1]), lambda i: (i, 0)),
            pl.BlockSpec((tm, d), lambda i: (i, gb)),
            pl.BlockSpec((tm, d), lambda i: (i, gb + 1)),
            pl.BlockSpec((tm, d), lambda i: (i, 0)),
            _const_spec(mod.shape),
            _const_spec(ln_g.shape),
            _const_spec(ln_b.shape),
            _const_spec(wmo.shape),
            _const_spec(wpo.shape),
            _const_spec(wout.shape),
        ],
        out_specs=pl.BlockSpec((tm, d), lambda i: (i, 0)),
        out_shape=jax.ShapeDtypeStruct((s, d), F32),
        compiler_params=_params(("parallel",)),
        name="final",
    )(ya, yp, g, g, x2, mod, ln_g, ln_b, wmo, wpo, wout)


def _swap_halves(w):
    half = QK_ROPE_DIM // 2
    return jnp.concatenate([w[..., half:], w[..., :half]], axis=-1)


def _prep_kernel(w_ref, wlat_ref, wg_ref):
    o_kv = Q_LORA_RANK + KV_LORA_RANK
    o_g = o_kv + QK_ROPE_DIM
    half = QK_ROPE_DIM // 2
    wlat_ref[0:o_g, :] = w_ref[0:o_g, :].astype(BF16)
    wlat_ref[o_g:o_g + half, :] = w_ref[o_kv + half:o_g, :].astype(BF16)
    wlat_ref[o_g + half:o_g + 2 * half, :] = w_ref[o_kv:o_kv + half, :].astype(BF16)
    wg_ref[...] = w_ref[o_g:, :].astype(BF16)


def _prep_w_in(w_t, tc=256):
    n, d = w_t.shape
    n_lat = Q_LORA_RANK + KV_LORA_RANK + 2 * QK_ROPE_DIM
    n_g = n - n_lat + QK_ROPE_DIM
    return pl.pallas_call(
        _prep_kernel,
        grid=(d // tc,),
        in_specs=[pl.BlockSpec((n, tc), lambda i: (0, i))],
        out_specs=[pl.BlockSpec((n_lat, tc), lambda i: (0, i)),
                   pl.BlockSpec((n_g, tc), lambda i: (0, i))],
        out_shape=[jax.ShapeDtypeStruct((n_lat, d), BF16), jax.ShapeDtypeStruct((n_g, d), BF16)],
        compiler_params=_params(("parallel",)),
        name="prep",
    )(w_t)


def kernel(x, c, positions, w_ada, b_ada, w_in, b_gates, q_norm_g, w_q_b, kv_norm_g, w_kv_b,
           w_mla_o, w_pool_g, pool_scale, w_pool_o, w_out, ln_g, ln_b):
    b, s, d = x.shape
    assert b == 1 and w_ada.shape[0] == 1
    l = 0
    x2 = x.reshape(s, d)

    wlat_t, wg_t = _prep_w_in(w_in[l].T)
    bg = jnp.concatenate([jnp.zeros((d + 2 * POOL_WIDTH,), F32), b_gates[l]]).reshape(1, -1)
    wq3 = w_q_b[l].reshape(Q_LORA_RANK, N_HEADS, QK_HEAD_DIM)
    wq_pe = wq3[..., QK_NOPE_DIM:]
    wq = jnp.concatenate([wq3[..., :QK_NOPE_DIM], wq_pe, _swap_halves(wq_pe)], axis=-1)
    wqt = jnp.transpose(wq, (1, 2, 0)).astype(BF16)
    wkv3 = w_kv_b[l].reshape(KV_LORA_RANK, N_HEADS, QK_NOPE_DIM + V_HEAD_DIM)
    wk = wkv3[..., :QK_NOPE_DIM].reshape(KV_LORA_RANK, N_HEADS * QK_NOPE_DIM).astype(BF16)
    wvt = jnp.transpose(wkv3[..., QK_NOPE_DIM:], (1, 2, 0)).reshape(
        N_HEADS * V_HEAD_DIM, KV_LORA_RANK).astype(BF16)
    inv_freq = ROPE_THETA ** (-jnp.arange(0, QK_ROPE_DIM, 2, dtype=F32) / QK_ROPE_DIM)
    freq = jnp.tile(inv_freq, 4).reshape(1, 128)
    pos_col = positions.reshape(s, 1)

    mod = _mod(c.reshape(d, 1), w_ada[l], b_ada[l].reshape(1, -1))
    h = _ln(x2, mod)
    qt, k, vt = _lat(h, wlat_t, q_norm_g[l].reshape(1, -1), kv_norm_g[l].reshape(1, -1),
                     wqt, wk, wvt, pos_col, freq)
    g = _gates(h, wg_t, bg)
    ya = _attn(qt, k, vt, g)
    yp = _pool(g, w_pool_g[l].astype(BF16), pool_scale[l].reshape(1, -1), d)
    out = _final(ya, yp, g, x2, mod, ln_g[l].reshape(1, -1), ln_b[l].reshape(1, -1),
                 w_mla_o[l].astype(BF16), w_pool_o[l].astype(BF16), w_out[l].astype(BF16))
    return out.reshape(b, s, d)
```

```python
import functools
import math

import jax
import jax.numpy as jnp
from jax import lax
from jax.experimental import pallas as pl
from jax.experimental.pallas import tpu as pltpu

N_HEADS = 16
QK_NOPE_DIM = 128
QK_ROPE_DIM = 64
QK_HEAD_DIM = QK_NOPE_DIM + QK_ROPE_DIM
V_HEAD_DIM = 128
Q_LORA_RANK = 512
KV_LORA_RANK = 512
ROPE_THETA = 10000.0
POOL_WINDOWS = (2, 4, 8, 16)
POOL_GROUP_DIM = 256
POOL_WIDTH = len(POOL_WINDOWS) * POOL_GROUP_DIM
RMS_EPS = 1e-6
LN_EPS = 1e-5
DEPTH = 1
DEEPNORM_ALPHA = (2.0 * DEPTH) ** 0.25

QK_PAD_DIM = 256
VT_ROWS = V_HEAD_DIM + 16
POOL_HALO = 16
VMEM_LIMIT = 56 * 1024 * 1024

F32 = jnp.float32
BF16 = jnp.bfloat16
_NT_DIMS = (((1,), (1,)), ((), ()))


def _const_spec(shape):
    nd = len(shape)
    return pl.BlockSpec(shape, lambda *_: (0,) * nd, pipeline_mode=pl.Buffered(1))


def _params(sem):
    return pltpu.CompilerParams(dimension_semantics=sem, vmem_limit_bytes=VMEM_LIMIT)


def _mod_kernel(c_ref, w_ref, b_ref, o_ref):
    c = c_ref[...]
    sc = c * jax.nn.sigmoid(c)
    o_ref[...] = jnp.sum(sc * w_ref[...], axis=0, keepdims=True) + b_ref[...]


def _mod(c_col, w_ada, b_ada, tn=512):
    d, n = w_ada.shape
    return pl.pallas_call(
        _mod_kernel,
        grid=(n // tn,),
        in_specs=[
            pl.BlockSpec((d, 1), lambda j: (0, 0)),
            pl.BlockSpec((d, tn), lambda j: (0, j)),
            pl.BlockSpec((1, tn), lambda j: (0, j)),
        ],
        out_specs=pl.BlockSpec((1, tn), lambda j: (0, j)),
        out_shape=jax.ShapeDtypeStruct((1, n), F32),
        compiler_params=_params(("arbitrary",)),
        name="mod",
    )(c_col, w_ada, b_ada)


def _ln_kernel(x_ref, mod_ref, h_ref, *, d):
    x = x_ref[...]
    mu = jnp.mean(x, axis=-1, keepdims=True)
    xc = x - mu
    var = jnp.mean(xc * xc, axis=-1, keepdims=True)
    y = xc * lax.rsqrt(var + LN_EPS)
    shift = mod_ref[:, 0:d]
    scale = mod_ref[:, d:2 * d]
    h_ref[...] = (y * (1.0 + scale) + shift).astype(h_ref.dtype)


def _ln(x2, mod, tm=512):
    s, d = x2.shape
    return pl.pallas_call(
        functools.partial(_ln_kernel, d=d),
        grid=(s // tm,),
        in_specs=[
            pl.BlockSpec((tm, d), lambda i: (i, 0)),
            pl.BlockSpec((1, 3 * d), lambda i: (0, 0)),
        ],
        out_specs=pl.BlockSpec((tm, d), lambda i: (i, 0)),
        out_shape=jax.ShapeDtypeStruct((s, d), BF16),
        compiler_params=_params(("parallel",)),
        name="ln",
    )(x2, mod)


def _rms(xf, g):
    return xf * lax.rsqrt(jnp.mean(xf * xf, axis=-1, keepdims=True) + RMS_EPS) * g


def _lat_kernel(h_ref, wlat_ref, qg_ref, kvg_ref, wqt_ref, wk_ref, wvt_ref, pos_ref, freq_ref,
                qt_ref, k_ref, vt_ref, *, q_scale):
    h = h_ref[...]
    lat = lax.dot_general(h, wlat_ref[...], _NT_DIMS, preferred_element_type=F32)
    qn = _rms(lat[:, 0:Q_LORA_RANK], qg_ref[...])
    cn = _rms(lat[:, Q_LORA_RANK:Q_LORA_RANK + KV_LORA_RANK], kvg_ref[...])
    kpe = lat[:, Q_LORA_RANK + KV_LORA_RANK:]
    qn_t = qn.T.astype(BF16)
    cn_t = cn.T.astype(BF16)
    cn = cn.astype(BF16)

    ang = pos_ref[...].astype(F32) * freq_ref[...]
    lane = lax.broadcasted_iota(jnp.int32, ang.shape, 1)
    cs = jnp.where(lane < 64, jnp.cos(ang),
                   jnp.where(lane < 96, -jnp.sin(ang), jnp.sin(ang)))
    cs_t = cs.T

    r = kpe * cs
    k_rot = jnp.where(lane < 64, r + pltpu.roll(r, 64, 1), 0.0).astype(BF16)
    zeros_t = jnp.zeros((QK_PAD_DIM - QK_HEAD_DIM, qn_t.shape[1]), BF16)
    ones_t = jnp.ones((VT_ROWS - V_HEAD_DIM, qn_t.shape[1]), BF16)
    for hd in range(N_HEADS):
        qh = jnp.dot(wqt_ref[hd], qn_t, preferred_element_type=F32)
        rq = qh[128:256, :] * cs_t
        qt_ref[hd, 0:128, :] = (qh[0:128, :] * q_scale).astype(BF16)
        qt_ref[hd, 128:192, :] = ((rq[0:64, :] + rq[64:128, :]) * q_scale).astype(BF16)
        qt_ref[hd, 192:256, :] = zeros_t
    for hp in range(N_HEADS // 2):
        kk = jnp.dot(cn, wk_ref[:, hp * 256:(hp + 1) * 256], preferred_element_type=F32)
        vv = jnp.dot(wvt_ref[hp * 256:(hp + 1) * 256, :], cn_t, preferred_element_type=F32)
        for e in range(2):
            hd = 2 * hp + e
            k_ref[hd, :, 0:128] = kk[:, e * 128:(e + 1) * 128].astype(BF16)
            k_ref[hd, :, 128:256] = k_rot
            vt_ref[hd, 0:V_HEAD_DIM, :] = vv[e * 128:(e + 1) * 128, :].astype(BF16)
            vt_ref[hd, V_HEAD_DIM:VT_ROWS, :] = ones_t


def _lat(h, wlat, qg, kvg, wqt, wk, wvt, pos_col, freq, tm=256):
    s, d = h.shape
    q_scale = (QK_HEAD_DIM ** -0.5) * math.log2(math.e)
    return pl.pallas_call(
        functools.partial(_lat_kernel, q_scale=q_scale),
        grid=(s // tm,),
        in_specs=[
            pl.BlockSpec((tm, d), lambda i: (i, 0)),
            _const_spec(wlat.shape),
            _const_spec(qg.shape),
            _const_spec(kvg.shape),
            _const_spec(wqt.shape),
            _const_spec(wk.shape),
            _const_spec(wvt.shape),
            pl.BlockSpec((tm, 1), lambda i: (i, 0)),
            _const_spec(freq.shape),
        ],
        out_specs=[
            pl.BlockSpec((N_HEADS, QK_PAD_DIM, tm), lambda i: (0, 0, i)),
            pl.BlockSpec((N_HEADS, tm, QK_PAD_DIM), lambda i: (0, i, 0)),
            pl.BlockSpec((N_HEADS, VT_ROWS, tm), lambda i: (0, 0, i)),
        ],
        out_shape=[
            jax.ShapeDtypeStruct((N_HEADS, QK_PAD_DIM, s), BF16),
            jax.ShapeDtypeStruct((N_HEADS, s, QK_PAD_DIM), BF16),
            jax.ShapeDtypeStruct((N_HEADS, VT_ROWS, s), BF16),
        ],
        compiler_params=_params(("parallel",)),
        name="lat",
    )(h, wlat, qg, kvg, wqt, wk, wvt, pos_col, freq)


def _gates_kernel(h_ref, w_ref, b_ref, o_ref, *, tn, cn, d, pool_w):
    j = pl.program_id(1)
    h = h_ref[...]
    for c in range(tn // cn):
        cols = slice(c * cn, (c + 1) * cn)
        col = j * tn + c * cn
        is_ident = jnp.logical_and(col >= d, col < d + pool_w)
        is_sigm = col >= d + 2 * pool_w
        z = lax.dot_general(h, w_ref[cols, :], _NT_DIMS, preferred_element_type=F32)
        sg = jax.nn.sigmoid(z + b_ref[:, cols])
        o_ref[:, cols] = jnp.where(is_ident, z, jnp.where(is_sigm, sg, z * sg)).astype(o_ref.dtype)


def _gates(h, wg_t, bg, tm=1024, tn=2048, cn=512):
    s, d = h.shape
    n = wg_t.shape[0]
    return pl.pallas_call(
        functools.partial(_gates_kernel, tn=tn, cn=cn, d=d, pool_w=POOL_WIDTH),
        grid=(s // tm, n // tn),
        in_specs=[
            pl.BlockSpec((tm, d), lambda i, j: (i, 0)),
            pl.BlockSpec((tn, d), lambda i, j: (j, 0)),
            pl.BlockSpec((1, tn), lambda i, j: (0, j)),
        ],
        out_specs=pl.BlockSpec((tm, tn), lambda i, j: (i, j)),
        out_shape=jax.ShapeDtypeStruct((s, n), BF16),
        compiler_params=_params(("parallel", "arbitrary")),
        name="gates",
    )(h, wg_t, bg)


def _attn_kernel(qt_ref, k_ref, vt_ref, sg_ref, o_ref, sa_sc, sb_sc, m_sc, acc_sc,
                 *, bq, bk, cw, nq):
    qi = pl.program_id(1)
    q_cur = pl.multiple_of(qi * bq, bq)
    q_nxt = pl.multiple_of(jnp.minimum(qi + 1, nq - 1) * bq, bq)
    m_sc[...] = jnp.full(m_sc.shape, -jnp.inf, F32)
    acc_sc[...] = jnp.zeros(acc_sc.shape, F32)

    def qk(q_start, j, s_sc, first_col=0):
        start = pl.multiple_of(j * bk, bk)
        q_t = qt_ref[0, :, pl.ds(pl.multiple_of(q_start + first_col, cw), bq - first_col)]
        s_sc[:, first_col:bq] = jnp.dot(k_ref[0, pl.ds(start, bk), :], q_t,
                                        preferred_element_type=F32)

    def softmax_pv(j, s_sc, diag):
        start = pl.multiple_of(j * bk, bk)
        for c in range(bq // cw):
            cols = slice(c * cw, (c + 1) * cw)
            if diag is None:
                nrow, masked = bk, False
            else:
                nrow = min(max((c + 1) * cw - diag * bk, 0), bk)
                masked = diag * bk + nrow - 1 > c * cw
            if nrow == 0:
                continue
            s = s_sc[0:nrow, cols]
            if masked:
                key = diag * bk + lax.broadcasted_iota(jnp.int32, s.shape, 0)
                qry = c * cw + lax.broadcasted_iota(jnp.int32, s.shape, 1)
                s = jnp.where(key <= qry, s, -jnp.inf)
            m_prev = m_sc[:, cols]
            m_new = jnp.maximum(m_prev, jnp.max(s, axis=0, keepdims=True))
            alpha = jnp.exp2(m_prev - m_new)
            p = jnp.exp2(s - m_new[0:1, :])
            vt = vt_ref[0, :, pl.ds(start, nrow)]
            pv = jnp.dot(vt, p.astype(BF16), preferred_element_type=F32)
            acc_sc[:, cols] = alpha[0:1, :] * acc_sc[:, cols] + pv
            m_sc[:, cols] = m_new

    @pl.when(qi == 0)
    def _():
        qk(q_cur, 0, sa_sc)

    def pair(a):
        qk(q_cur, a + 1, sb_sc)
        softmax_pv(a, sa_sc, None)
        qk(q_cur, a + 2, sa_sc)
        softmax_pv(a + 1, sb_sc, None)

    def four_pairs(t, carry):
        for u in range(4):
            pair(8 * t + 2 * u)
        return carry

    def two_pairs(t, carry):
        pair(4 * t)
        pair(4 * t + 2)
        return carry

    def one_pair(t, carry):
        pair(2 * t)
        return carry

    n_quad = lax.shift_right_logical(qi, 2)
    n_duo = lax.shift_right_logical(qi, 1)
    lax.fori_loop(0, n_quad, four_pairs, 0)
    lax.fori_loop(2 * n_quad, n_duo, two_pairs, 0)
    lax.fori_loop(2 * n_duo, qi, one_pair, 0)

    a = 2 * qi
    qk(q_cur, a + 1, sb_sc, first_col=bk)
    softmax_pv(a, sa_sc, 0)
    qk(q_nxt, 0, sa_sc)
    softmax_pv(a + 1, sb_sc, 1)

    for c in range(bq // cw):
        cols = slice(c * cw, (c + 1) * cw)
        o_t = acc_sc[0:V_HEAD_DIM, cols] / acc_sc[V_HEAD_DIM:V_HEAD_DIM + 1, cols]
        o_ref[cols, :] = (o_t.T * sg_ref[cols, :].astype(F32)).astype(o_ref.dtype)


def _attn(qt, k, vt, g, bq=1024, cw=256):
    nh, _, s = qt.shape
    bk = bq // 2
    nq = s // bq
    return pl.pallas_call(
        functools.partial(_attn_kernel, bq=bq, bk=bk, cw=cw, nq=nq),
        grid=(nh, nq),
        in_specs=[
            pl.BlockSpec((1, QK_PAD_DIM, s), lambda h, i: (h, 0, 0)),
            pl.BlockSpec((1, s, QK_PAD_DIM), lambda h, i: (h, 0, 0)),
            pl.BlockSpec((1, VT_ROWS, s), lambda h, i: (h, 0, 0)),
            pl.BlockSpec((bq, V_HEAD_DIM), lambda h, i: (i, h)),
        ],
        out_specs=pl.BlockSpec((bq, V_HEAD_DIM), lambda h, i: (i, h)),
        out_shape=jax.ShapeDtypeStruct((s, nh * V_HEAD_DIM), BF16),
        scratch_shapes=[
            pltpu.VMEM((bk, bq), F32),
            pltpu.VMEM((bk, bq), F32),
            pltpu.VMEM((8, bq), F32),
            pltpu.VMEM((VT_ROWS, bq), F32),
        ],
        compiler_params=_params(("arbitrary", "arbitrary")),
        name="attn",
    )(qt, k, vt, g)


def _pool_kernel(u_ref, halo_ref, pg_ref, w_ref, sc_ref, o_ref, *, tm):
    i = pl.program_id(0)
    u = u_ref[...].astype(F32)
    halo = jnp.where(i > 0, halo_ref[...].astype(F32), 0.0)
    ext = jnp.concatenate([halo, u], axis=0)
    t = i * tm + lax.broadcasted_iota(jnp.int32, (tm, 1), 0)
    n_ext = POOL_HALO + tm

    def shifted(a, kk):
        return jnp.concatenate([jnp.zeros((kk, a.shape[1]), F32), a[:a.shape[0] - kk]], axis=0)

    acc = ext
    width = 1
    for g, w in enumerate(POOL_WINDOWS):
        lo = g * POOL_GROUP_DIM
        acc = acc[:, (0 if g == 0 else POOL_GROUP_DIM):]
        while width < w:
            acc = acc + shifted(acc, width)
            width *= 2
        win = acc[POOL_HALO:n_ext, 0:POOL_GROUP_DIM]
        cnt = jnp.minimum(t + 1, w).astype(F32)
        pooled = win / cnt - u[:, lo:lo + POOL_GROUP_DIM]
        mixed = jnp.dot(pooled.astype(BF16), w_ref[g], preferred_element_type=F32)
        mixed = mixed * sc_ref[:, lo:lo + POOL_GROUP_DIM]
        o_ref[:, lo:lo + POOL_GROUP_DIM] = (
            mixed * pg_ref[:, lo:lo + POOL_GROUP_DIM].astype(F32)).astype(o_ref.dtype)


def _pool(g, w_pool, pool_scale, d, tm=512):
    s = g.shape[0]
    p = POOL_WIDTH
    ub = d // p
    hb = tm // POOL_HALO
    return pl.pallas_call(
        functools.partial(_pool_kernel, tm=tm),
        grid=(s // tm,),
        in_specs=[
            pl.BlockSpec((tm, p), lambda i: (i, ub)),
            pl.BlockSpec((POOL_HALO, p), lambda i: (jnp.maximum(i * hb - 1, 0), ub)),
            pl.BlockSpec((tm, p), lambda i: (i, ub + 1)),
            _const_spec(w_pool.shape),
            _const_spec(pool_scale.shape),
        ],
        out_specs=pl.BlockSpec((tm, p), lambda i: (i, 0)),
        out_shape=jax.ShapeDtypeStruct((s, p), BF16),
        compiler_params=_params(("parallel",)),
        name="pool",
    )(g, g, g, w_pool, pool_scale)


def _final_kernel(ya_ref, yp_ref, gm_ref, gp_ref, x_ref, mod_ref, lng_ref, lnb_ref,
                  wmo_ref, wpo_ref, wout_ref, o_ref, *, d):
    y_mla = jnp.dot(ya_ref[...], wmo_ref[...], preferred_element_type=F32)
    y_pool = jnp.dot(yp_ref[...], wpo_ref[...], preferred_element_type=F32)
    merged = gm_ref[...].astype(F32) * y_mla + gp_ref[...].astype(F32) * y_pool
    y = jnp.dot(merged.astype(BF16), wout_ref[...], preferred_element_type=F32)
    gate = mod_ref[:, 2 * d:3 * d]
    r = DEEPNORM_ALPHA * x_ref[...] + (1.0 + gate) * y
    mu = jnp.mean(r, axis=-1, keepdims=True)
    rc = r - mu
    var = jnp.mean(rc * rc, axis=-1, keepdims=True)
    o_ref[...] = rc * lax.rsqrt(var + LN_EPS) * lng_ref[...] + lnb_ref[...]


def _final(ya, yp, g, x2, mod, ln_g, ln_b, wmo, wpo, wout, tm=512):
    s, d = x2.shape
    gb = (d + 2 * POOL_WIDTH) // d
    return pl.pallas_call(
        functools.partial(_final_kernel, d=d),
        grid=(s // tm,),
        in_specs=[
            pl.BlockSpec((tm, ya.shape[1]), lambda i: (i, 0)),
            pl.BlockSpec((tm, yp.shape[1]), lambda i: (i, 0)),
            pl.BlockSpec((tm, d), lambda i: (i, gb)),
            pl.BlockSpec((tm, d), lambda i: (i, gb + 1)),
            pl.BlockSpec((tm, d), lambda i: (i, 0)),
            _const_spec(mod.shape),
            _const_spec(ln_g.shape),
            _const_spec(ln_b.shape),
            _const_spec(wmo.shape),
            _const_spec(wpo.shape),
            _const_spec(wout.shape),
        ],
        out_specs=pl.BlockSpec((tm, d), lambda i: (i, 0)),
        out_shape=jax.ShapeDtypeStruct((s, d), F32),
        compiler_params=_params(("parallel",)),
        name="final",
    )(ya, yp, g, g, x2, mod, ln_g, ln_b, wmo, wpo, wout)


def _swap_halves(w):
    half = QK_ROPE_DIM // 2
    return jnp.concatenate([w[..., half:], w[..., :half]], axis=-1)


def _prep_kernel(w_ref, wlat_ref, wg_ref):
    o_kv = Q_LORA_RANK + KV_LORA_RANK
    o_g = o_kv + QK_ROPE_DIM
    half = QK_ROPE_DIM // 2
    wlat_ref[0:o_g, :] = w_ref[0:o_g, :].astype(BF16)
    wlat_ref[o_g:o_g + half, :] = w_ref[o_kv + half:o_g, :].astype(BF16)
    wlat_ref[o_g + half:o_g + 2 * half, :] = w_ref[o_kv:o_kv + half, :].astype(BF16)
    wg_ref[...] = w_ref[o_g:, :].astype(BF16)


def _prep_w_in(w_t, tc=256):
    n, d = w_t.shape
    n_lat = Q_LORA_RANK + KV_LORA_RANK + 2 * QK_ROPE_DIM
    n_g = n - n_lat + QK_ROPE_DIM
    return pl.pallas_call(
        _prep_kernel,
        grid=(d // tc,),
        in_specs=[pl.BlockSpec((n, tc), lambda i: (0, i))],
        out_specs=[pl.BlockSpec((n_lat, tc), lambda i: (0, i)),
                   pl.BlockSpec((n_g, tc), lambda i: (0, i))],
        out_shape=[jax.ShapeDtypeStruct((n_lat, d), BF16), jax.ShapeDtypeStruct((n_g, d), BF16)],
        compiler_params=_params(("parallel",)),
        name="prep",
    )(w_t)


def kernel(x, c, positions, w_ada, b_ada, w_in, b_gates, q_norm_g, w_q_b, kv_norm_g, w_kv_b,
           w_mla_o, w_pool_g, pool_scale, w_pool_o, w_out, ln_g, ln_b):
    b, s, d = x.shape
    assert b == 1 and w_ada.shape[0] == 1
    l = 0
    x2 = x.reshape(s, d)

    wlat_t, wg_t = _prep_w_in(w_in[l].T)
    bg = jnp.concatenate([jnp.zeros((d + 2 * POOL_WIDTH,), F32), b_gates[l]]).reshape(1, -1)
    wq3 = w_q_b[l].reshape(Q_LORA_RANK, N_HEADS, QK_HEAD_DIM)
    wq_pe = wq3[..., QK_NOPE_DIM:]
    wq = jnp.concatenate([wq3[..., :QK_NOPE_DIM], wq_pe, _swap_halves(wq_pe)], axis=-1)
    wqt = jnp.transpose(wq, (1, 2, 0)).astype(BF16)
    wkv3 = w_kv_b[l].reshape(KV_LORA_RANK, N_HEADS, QK_NOPE_DIM + V_HEAD_DIM)
    wk = wkv3[..., :QK_NOPE_DIM].reshape(KV_LORA_RANK, N_HEADS * QK_NOPE_DIM).astype(BF16)
    wvt = jnp.transpose(wkv3[..., QK_NOPE_DIM:], (1, 2, 0)).reshape(
        N_HEADS * V_HEAD_DIM, KV_LORA_RANK).astype(BF16)
    inv_freq = ROPE_THETA ** (-jnp.arange(0, QK_ROPE_DIM, 2, dtype=F32) / QK_ROPE_DIM)
    freq = jnp.tile(inv_freq, 4).reshape(1, 128)
    pos_col = positions.reshape(s, 1)

    mod = _mod(c.reshape(d, 1), w_ada[l], b_ada[l].reshape(1, -1))
    h = _ln(x2, mod)
    qt, k, vt = _lat(h, wlat_t, q_norm_g[l].reshape(1, -1), kv_norm_g[l].reshape(1, -1),
                     wqt, wk, wvt, pos_col, freq)
    g = _gates(h, wg_t, bg)
    ya = _attn(qt, k, vt, g)
    yp = _pool(g, w_pool_g[l].astype(BF16), pool_scale[l].reshape(1, -1), d)
    out = _final(ya, yp, g, x2, mod, ln_g[l].reshape(1, -1), ln_b[l].reshape(1, -1),
                 w_mla_o[l].astype(BF16), w_pool_o[l].astype(BF16), w_out[l].astype(BF16))
    return out.reshape(b, s, d)
```

```python
import functools
import math

import jax
import jax.numpy as jnp
from jax import lax
from jax.experimental import pallas as pl
from jax.experimental.pallas import tpu as pltpu

N_HEADS = 16
QK_NOPE_DIM = 128
QK_ROPE_DIM = 64
QK_HEAD_DIM = QK_NOPE_DIM + QK_ROPE_DIM
V_HEAD_DIM = 128
Q_LORA_RANK = 512
KV_LORA_RANK = 512
ROPE_THETA = 10000.0
POOL_WINDOWS = (2, 4, 8, 16)
POOL_GROUP_DIM = 256
POOL_WIDTH = len(POOL_WINDOWS) * POOL_GROUP_DIM
RMS_EPS = 1e-6
LN_EPS = 1e-5
DEPTH = 1
DEEPNORM_ALPHA = (2.0 * DEPTH) ** 0.25

QK_PAD_DIM = 256
VT_ROWS = V_HEAD_DIM + 16
POOL_HALO = 16
VMEM_LIMIT = 56 * 1024 * 1024

F32 = jnp.float32
BF16 = jnp.bfloat16
_NT_DIMS = (((1,), (1,)), ((), ()))


def _const_spec(shape):
    nd = len(shape)
    return pl.BlockSpec(shape, lambda *_: (0,) * nd, pipeline_mode=pl.Buffered(1))


def _params(sem):
    return pltpu.CompilerParams(dimension_semantics=sem, vmem_limit_bytes=VMEM_LIMIT)


def _mod_kernel(c_ref, w_ref, b_ref, o_ref):
    c = c_ref[...]
    sc = c * jax.nn.sigmoid(c)
    o_ref[...] = jnp.sum(sc * w_ref[...], axis=0, keepdims=True) + b_ref[...]


def _mod(c_col, w_ada, b_ada, tn=512):
    d, n = w_ada.shape
    return pl.pallas_call(
        _mod_kernel,
        grid=(n // tn,),
        in_specs=[
            pl.BlockSpec((d, 1), lambda j: (0, 0)),
            pl.BlockSpec((d, tn), lambda j: (0, j)),
            pl.BlockSpec((1, tn), lambda j: (0, j)),
        ],
        out_specs=pl.BlockSpec((1, tn), lambda j: (0, j)),
        out_shape=jax.ShapeDtypeStruct((1, n), F32),
        compiler_params=_params(("arbitrary",)),
        name="mod",
    )(c_col, w_ada, b_ada)


def _ln_kernel(x_ref, mod_ref, h_ref, *, d):
    x = x_ref[...]
    mu = jnp.mean(x, axis=-1, keepdims=True)
    xc = x - mu
    var = jnp.mean(xc * xc, axis=-1, keepdims=True)
    y = xc * lax.rsqrt(var + LN_EPS)
    shift = mod_ref[:, 0:d]
    scale = mod_ref[:, d:2 * d]
    h_ref[...] = (y * (1.0 + scale) + shift).astype(h_ref.dtype)


def _ln(x2, mod, tm=512):
    s, d = x2.shape
    return pl.pallas_call(
        functools.partial(_ln_kernel, d=d),
        grid=(s // tm,),
        in_specs=[
            pl.BlockSpec((tm, d), lambda i: (i, 0)),
            pl.BlockSpec((1, 3 * d), lambda i: (0, 0)),
        ],
        out_specs=pl.BlockSpec((tm, d), lambda i: (i, 0)),
        out_shape=jax.ShapeDtypeStruct((s, d), BF16),
        compiler_params=_params(("parallel",)),
        name="ln",
    )(x2, mod)


def _rms(xf, g):
    return xf * lax.rsqrt(jnp.mean(xf * xf, axis=-1, keepdims=True) + RMS_EPS) * g


def _lat_kernel(h_ref, wlat_ref, qg_ref, kvg_ref, wqt_ref, wk_ref, wvt_ref, pos_ref, freq_ref,
                qt_ref, k_ref, vt_ref, *, q_scale):
    h = h_ref[...]
    lat = lax.dot_general(h, wlat_ref[...], _NT_DIMS, preferred_element_type=F32)
    qn = _rms(lat[:, 0:Q_LORA_RANK], qg_ref[...])
    cn = _rms(lat[:, Q_LORA_RANK:Q_LORA_RANK + KV_LORA_RANK], kvg_ref[...])
    kpe = lat[:, Q_LORA_RANK + KV_LORA_RANK:]
    qn_t = qn.T.astype(BF16)
    cn_t = cn.T.astype(BF16)
    cn = cn.astype(BF16)

    ang = pos_ref[...].astype(F32) * freq_ref[...]
    lane = lax.broadcasted_iota(jnp.int32, ang.shape, 1)
    cs = jnp.where(lane < 64, jnp.cos(ang),
                   jnp.where(lane < 96, -jnp.sin(ang), jnp.sin(ang)))
    cs_t = cs.T

    r = kpe * cs
    k_rot = jnp.where(lane < 64, r + pltpu.roll(r, 64, 1), 0.0).astype(BF16)
    zeros_t = jnp.zeros((QK_PAD_DIM - QK_HEAD_DIM, qn_t.shape[1]), BF16)
    ones_t = jnp.ones((VT_ROWS - V_HEAD_DIM, qn_t.shape[1]), BF16)
    for hd in range(N_HEADS):
        qh = jnp.dot(wqt_ref[hd], qn_t, preferred_element_type=F32)
        rq = qh[128:256, :] * cs_t
        qt_ref[hd, 0:128, :] = (qh[0:128, :] * q_scale).astype(BF16)
        qt_ref[hd, 128:192, :] = ((rq[0:64, :] + rq[64:128, :]) * q_scale).astype(BF16)
        qt_ref[hd, 192:256, :] = zeros_t
    for hp in range(N_HEADS // 2):
        kk = jnp.dot(cn, wk_ref[:, hp * 256:(hp + 1) * 256], preferred_element_type=F32)
        vv = jnp.dot(wvt_ref[hp * 256:(hp + 1) * 256, :], cn_t, preferred_element_type=F32)
        for e in range(2):
            hd = 2 * hp + e
            k_ref[hd, :, 0:128] = kk[:, e * 128:(e + 1) * 128].astype(BF16)
            k_ref[hd, :, 128:256] = k_rot
            vt_ref[hd, 0:V_HEAD_DIM, :] = vv[e * 128:(e + 1) * 128, :].astype(BF16)
            vt_ref[hd, V_HEAD_DIM:VT_ROWS, :] = ones_t


def _lat(h, wlat, qg, kvg, wqt, wk, wvt, pos_col, freq, tm=512):
    s, d = h.shape
    q_scale = (QK_HEAD_DIM ** -0.5) * math.log2(math.e)
    return pl.pallas_call(
        functools.partial(_lat_kernel, q_scale=q_scale),
        grid=(s // tm,),
        in_specs=[
            pl.BlockSpec((tm, d), lambda i: (i, 0)),
            _const_spec(wlat.shape),
            _const_spec(qg.shape),
            _const_spec(kvg.shape),
            _const_spec(wqt.shape),
            _const_spec(wk.shape),
            _const_spec(wvt.shape),
            pl.BlockSpec((tm, 1), lambda i: (i, 0)),
            _const_spec(freq.shape),
        ],
        out_specs=[
            pl.BlockSpec((N_HEADS, QK_PAD_DIM, tm), lambda i: (0, 0, i)),
            pl.BlockSpec((N_HEADS, tm, QK_PAD_DIM), lambda i: (0, i, 0)),
            pl.BlockSpec((N_HEADS, VT_ROWS, tm), lambda i: (0, 0, i)),
        ],
        out_shape=[
            jax.ShapeDtypeStruct((N_HEADS, QK_PAD_DIM, s), BF16),
            jax.ShapeDtypeStruct((N_HEADS, s, QK_PAD_DIM), BF16),
            jax.ShapeDtypeStruct((N_HEADS, VT_ROWS, s), BF16),
        ],
        compiler_params=_params(("parallel",)),
        name="lat",
    )(h, wlat, qg, kvg, wqt, wk, wvt, pos_col, freq)


def _gates_kernel(h_ref, w_ref, b_ref, o_ref, *, tn, cn, d, pool_w):
    j = pl.program_id(1)
    h = h_ref[...]
    for c in range(tn // cn):
        cols = slice(c * cn, (c + 1) * cn)
        col = j * tn + c * cn
        is_ident = jnp.logical_and(col >= d, col < d + pool_w)
        is_sigm = col >= d + 2 * pool_w
        z = lax.dot_general(h, w_ref[cols, :], _NT_DIMS, preferred_element_type=F32)
        sg = jax.nn.sigmoid(z + b_ref[:, cols])
        o_ref[:, cols] = jnp.where(is_ident, z, jnp.where(is_sigm, sg, z * sg)).astype(o_ref.dtype)


def _gates(h, wg_t, bg, tm=1024, tn=2048, cn=512):
    s, d = h.shape
    n = wg_t.shape[0]
    return pl.pallas_call(
        functools.partial(_gates_kernel, tn=tn, cn=cn, d=d, pool_w=POOL_WIDTH),
        grid=(s // tm, n // tn),
        in_specs=[
            pl.BlockSpec((tm, d), lambda i, j: (i, 0)),
            pl.BlockSpec((tn, d), lambda i, j: (j, 0)),
            pl.BlockSpec((1, tn), lambda i, j: (0, j)),
        ],
        out_specs=pl.BlockSpec((tm, tn), lambda i, j: (i, j)),
        out_shape=jax.ShapeDtypeStruct((s, n), BF16),
        compiler_params=_params(("parallel", "arbitrary")),
        name="gates",
    )(h, wg_t, bg)


def _attn_kernel(qt_ref, k_ref, vt_ref, sg_ref, o_ref, sa_sc, sb_sc, m_sc, acc_sc,
                 *, bq, bk, cw, nq):
    qi = pl.program_id(1)
    q_cur = pl.multiple_of(qi * bq, bq)
    q_nxt = pl.multiple_of(jnp.minimum(qi + 1, nq - 1) * bq, bq)
    m_sc[...] = jnp.full(m_sc.shape, -jnp.inf, F32)
    acc_sc[...] = jnp.zeros(acc_sc.shape, F32)

    def qk(q_start, j, s_sc, first_col=0):
        start = pl.multiple_of(j * bk, bk)
        q_t = qt_ref[0, :, pl.ds(pl.multiple_of(q_start + first_col, cw), bq - first_col)]
        s_sc[:, first_col:bq] = jnp.dot(k_ref[0, pl.ds(start, bk), :], q_t,
                                        preferred_element_type=F32)

    def softmax_pv(j, s_sc, diag):
        start = pl.multiple_of(j * bk, bk)
        for c in range(bq // cw):
            cols = slice(c * cw, (c + 1) * cw)
            if diag is None:
                nrow, masked = bk, False
            else:
                nrow = min(max((c + 1) * cw - diag * bk, 0), bk)
                masked = diag * bk + nrow - 1 > c * cw
            if nrow == 0:
                continue
            def load_scores():
                s = s_sc[0:nrow, cols]
                if masked:
                    key = diag * bk + lax.broadcasted_iota(jnp.int32, s.shape, 0)
                    qry = c * cw + lax.broadcasted_iota(jnp.int32, s.shape, 1)
                    s = jnp.where(key <= qry, s, -jnp.inf)
                return s

            m_prev = m_sc[:, cols]
            m_new = jnp.maximum(m_prev, jnp.max(load_scores(), axis=0, keepdims=True))
            alpha = jnp.exp2(m_prev - m_new)
            m_sc[:, cols] = m_new
            p = jnp.exp2(load_scores() - m_new[0:1, :])
            vt = vt_ref[0, :, pl.ds(start, nrow)]
            pv = jnp.dot(vt, p.astype(BF16), preferred_element_type=F32)
            acc_sc[:, cols] = alpha[0:1, :] * acc_sc[:, cols] + pv

    @pl.when(qi == 0)
    def _():
        qk(q_cur, 0, sa_sc)

    def pair(a):
        qk(q_cur, a + 1, sb_sc)
        softmax_pv(a, sa_sc, None)
        qk(q_cur, a + 2, sa_sc)
        softmax_pv(a + 1, sb_sc, None)

    def four_pairs(t, carry):
        for u in range(4):
            pair(8 * t + 2 * u)
        return carry

    def two_pairs(t, carry):
        pair(4 * t)
        pair(4 * t + 2)
        return carry

    def one_pair(t, carry):
        pair(2 * t)
        return carry

    n_quad = lax.shift_right_logical(qi, 2)
    n_duo = lax.shift_right_logical(qi, 1)
    lax.fori_loop(0, n_quad, four_pairs, 0)
    lax.fori_loop(2 * n_quad, n_duo, two_pairs, 0)
    lax.fori_loop(2 * n_duo, qi, one_pair, 0)

    a = 2 * qi
    qk(q_cur, a + 1, sb_sc, first_col=bk)
    softmax_pv(a, sa_sc, 0)
    qk(q_nxt, 0, sa_sc)
    softmax_pv(a + 1, sb_sc, 1)

    for c in range(bq // cw):
        cols = slice(c * cw, (c + 1) * cw)
        o_t = acc_sc[0:V_HEAD_DIM, cols] / acc_sc[V_HEAD_DIM:V_HEAD_DIM + 1, cols]
        o_ref[cols, :] = (o_t.T * sg_ref[cols, :].astype(F32)).astype(o_ref.dtype)


def _attn(qt, k, vt, g, bq=1024, cw=256):
    nh, _, s = qt.shape
    bk = bq // 2
    nq = s // bq
    return pl.pallas_call(
        functools.partial(_attn_kernel, bq=bq, bk=bk, cw=cw, nq=nq),
        grid=(nh, nq),
        in_specs=[
            pl.BlockSpec((1, QK_PAD_DIM, s), lambda h, i: (h, 0, 0)),
            pl.BlockSpec((1, s, QK_PAD_DIM), lambda h, i: (h, 0, 0)),
            pl.BlockSpec((1, VT_ROWS, s), lambda h, i: (h, 0, 0)),
            pl.BlockSpec((bq, V_HEAD_DIM), lambda h, i: (i, h)),
        ],
        out_specs=pl.BlockSpec((bq, V_HEAD_DIM), lambda h, i: (i, h)),
        out_shape=jax.ShapeDtypeStruct((s, nh * V_HEAD_DIM), BF16),
        scratch_shapes=[
            pltpu.VMEM((bk, bq), F32),
            pltpu.VMEM((bk, bq), F32),
            pltpu.VMEM((8, bq), F32),
            pltpu.VMEM((VT_ROWS, bq), F32),
        ],
        compiler_params=_params(("arbitrary", "arbitrary")),
        name="attn",
    )(qt, k, vt, g)


def _pool_mix(u_ref, halo_ref, pg_ref, w_ref, sc_ref, i, tm):
    u = u_ref[...].astype(F32)
    halo = jnp.where(i > 0, halo_ref[...].astype(F32), 0.0)
    ext = jnp.concatenate([halo, u], axis=0)
    t = i * tm + lax.broadcasted_iota(jnp.int32, (tm, 1), 0)
    n_ext = POOL_HALO + tm

    def shifted(a, kk):
        return jnp.concatenate([jnp.zeros((kk, a.shape[1]), F32), a[:a.shape[0] - kk]], axis=0)

    acc = ext
    width = 1
    outs = []
    for g, w in enumerate(POOL_WINDOWS):
        lo = g * POOL_GROUP_DIM
        acc = acc[:, (0 if g == 0 else POOL_GROUP_DIM):]
        while width < w:
            acc = acc + shifted(acc, width)
            width *= 2
        win = acc[POOL_HALO:n_ext, 0:POOL_GROUP_DIM]
        cnt = jnp.minimum(t + 1, w).astype(F32)
        pooled = win / cnt - u[:, lo:lo + POOL_GROUP_DIM]
        mixed = jnp.dot(pooled.astype(BF16), w_ref[g], preferred_element_type=F32)
        mixed = mixed * sc_ref[:, lo:lo + POOL_GROUP_DIM]
        outs.append((mixed * pg_ref[:, lo:lo + POOL_GROUP_DIM].astype(F32)).astype(BF16))
    return jnp.concatenate(outs, axis=1)


def _final_kernel(ya_ref, u_ref, halo_ref, pg_ref, gm_ref, gp_ref, x_ref, mod_ref, lng_ref,
                  lnb_ref, wpg_ref, psc_ref, wmo_ref, wpo_ref, wout_ref, o_ref, *, d, tm):
    yp = _pool_mix(u_ref, halo_ref, pg_ref, wpg_ref, psc_ref, pl.program_id(0), tm)
    y_mla = jnp.dot(ya_ref[...], wmo_ref[...], preferred_element_type=F32)
    y_pool = jnp.dot(yp, wpo_ref[...], preferred_element_type=F32)
    merged = gm_ref[...].astype(F32) * y_mla + gp_ref[...].astype(F32) * y_pool
    y = jnp.dot(merged.astype(BF16), wout_ref[...], preferred_element_type=F32)
    gate = mod_ref[:, 2 * d:3 * d]
    r = DEEPNORM_ALPHA * x_ref[...] + (1.0 + gate) * y
    mu = jnp.mean(r, axis=-1, keepdims=True)
    rc = r - mu
    var = jnp.mean(rc * rc, axis=-1, keepdims=True)
    o_ref[...] = rc * lax.rsqrt(var + LN_EPS) * lng_ref[...] + lnb_ref[...]


def _final(ya, g, x2, mod, ln_g, ln_b, w_pool, pool_scale, wmo, wpo, wout, tm=256):
    s, d = x2.shape
    p = POOL_WIDTH
    gb = (d + 2 * p) // d
    ub = d // p
    hb = tm // POOL_HALO
    return pl.pallas_call(
        functools.partial(_final_kernel, d=d, tm=tm),
        grid=(s // tm,),
        in_specs=[
            pl.BlockSpec((tm, ya.shape[1]), lambda i: (i, 0)),
            pl.BlockSpec((tm, p), lambda i: (i, ub)),
            pl.BlockSpec((POOL_HALO, p), lambda i: (jnp.maximum(i * hb - 1, 0), ub)),
            pl.BlockSpec((tm, p), lambda i: (i, ub + 1)),
            pl.BlockSpec((tm, d), lambda i: (i, gb)),
            pl.BlockSpec((tm, d), lambda i: (i, gb + 1)),
            pl.BlockSpec((tm, d), lambda i: (i, 0)),
            _const_spec(mod.shape),
            _const_spec(ln_g.shape),
            _const_spec(ln_b.shape),
            _const_spec(w_pool.shape),
            _const_spec(pool_scale.shape),
            _const_spec(wmo.shape),
            _const_spec(wpo.shape),
            _const_spec(wout.shape),
        ],
        out_specs=pl.BlockSpec((tm, d), lambda i: (i, 0)),
        out_shape=jax.ShapeDtypeStruct((s, d), F32),
        compiler_params=_params(("parallel",)),
        name="final",
    )(ya, g, g, g, g, g, x2, mod, ln_g, ln_b, w_pool, pool_scale, wmo, wpo, wout)


def _swap_halves(w):
    half = QK_ROPE_DIM // 2
    return jnp.concatenate([w[..., half:], w[..., :half]], axis=-1)


def _prep_kernel(w_ref, wlat_ref, wg_ref):
    o_kv = Q_LORA_RANK + KV_LORA_RANK
    o_g = o_kv + QK_ROPE_DIM
    half = QK_ROPE_DIM // 2
    wlat_ref[0:o_g, :] = w_ref[0:o_g, :].astype(BF16)
    wlat_ref[o_g:o_g + half, :] = w_ref[o_kv + half:o_g, :].astype(BF16)
    wlat_ref[o_g + half:o_g + 2 * half, :] = w_ref[o_kv:o_kv + half, :].astype(BF16)
    wg_ref[...] = w_ref[o_g:, :].astype(BF16)


def _prep_w_in(w_t, tc=256):
    n, d = w_t.shape
    n_lat = Q_LORA_RANK + KV_LORA_RANK + 2 * QK_ROPE_DIM
    n_g = n - n_lat + QK_ROPE_DIM
    return pl.pallas_call(
        _prep_kernel,
        grid=(d // tc,),
        in_specs=[pl.BlockSpec((n, tc), lambda i: (0, i))],
        out_specs=[pl.BlockSpec((n_lat, tc), lambda i: (0, i)),
                   pl.BlockSpec((n_g, tc), lambda i: (0, i))],
        out_shape=[jax.ShapeDtypeStruct((n_lat, d), BF16), jax.ShapeDtypeStruct((n_g, d), BF16)],
        compiler_params=_params(("parallel",)),
        name="prep",
    )(w_t)


def kernel(x, c, positions, w_ada, b_ada, w_in, b_gates, q_norm_g, w_q_b, kv_norm_g, w_kv_b,
           w_mla_o, w_pool_g, pool_scale, w_pool_o, w_out, ln_g, ln_b):
    b, s, d = x.shape
    assert b == 1 and w_ada.shape[0] == 1
    l = 0
    x2 = x.reshape(s, d)

    wlat_t, wg_t = _prep_w_in(w_in[l].T)
    bg = jnp.concatenate([jnp.zeros((d + 2 * POOL_WIDTH,), F32), b_gates[l]]).reshape(1, -1)
    wq3 = w_q_b[l].reshape(Q_LORA_RANK, N_HEADS, QK_HEAD_DIM)
    wq_pe = wq3[..., QK_NOPE_DIM:]
    wq = jnp.concatenate([wq3[..., :QK_NOPE_DIM], wq_pe, _swap_halves(wq_pe)], axis=-1)
    wqt = jnp.transpose(wq, (1, 2, 0)).astype(BF16)
    wkv3 = w_kv_b[l].reshape(KV_LORA_RANK, N_HEADS, QK_NOPE_DIM + V_HEAD_DIM)
    wk = wkv3[..., :QK_NOPE_DIM].reshape(KV_LORA_RANK, N_HEADS * QK_NOPE_DIM).astype(BF16)
    wvt = jnp.transpose(wkv3[..., QK_NOPE_DIM:], (1, 2, 0)).reshape(
        N_HEADS * V_HEAD_DIM, KV_LORA_RANK).astype(BF16)
    inv_freq = ROPE_THETA ** (-jnp.arange(0, QK_ROPE_DIM, 2, dtype=F32) / QK_ROPE_DIM)
    freq = jnp.tile(inv_freq, 4).reshape(1, 128)
    pos_col = positions.reshape(s, 1)

    mod = _mod(c.reshape(d, 1), w_ada[l], b_ada[l].reshape(1, -1))
    h = _ln(x2, mod)
    qt, k, vt = _lat(h, wlat_t, q_norm_g[l].reshape(1, -1), kv_norm_g[l].reshape(1, -1),
                     wqt, wk, wvt, pos_col, freq)
    g = _gates(h, wg_t, bg)
    ya = _attn(qt, k, vt, g)
    out = _final(ya, g, x2, mod, ln_g[l].reshape(1, -1), ln_b[l].reshape(1, -1),
                 w_pool_g[l].astype(BF16), pool_scale[l].reshape(1, -1),
                 w_mla_o[l].astype(BF16), w_pool_o[l].astype(BF16), w_out[l].astype(BF16))
    return out.reshape(b, s, d)
```

```python
import functools
import math

import jax
import jax.numpy as jnp
from jax import lax
from jax.experimental import pallas as pl
from jax.experimental.pallas import tpu as pltpu

N_HEADS = 16
QK_NOPE_DIM = 128
QK_ROPE_DIM = 64
QK_HEAD_DIM = QK_NOPE_DIM + QK_ROPE_DIM
V_HEAD_DIM = 128
Q_LORA_RANK = 512
KV_LORA_RANK = 512
ROPE_THETA = 10000.0
POOL_WINDOWS = (2, 4, 8, 16)
POOL_GROUP_DIM = 256
POOL_WIDTH = len(POOL_WINDOWS) * POOL_GROUP_DIM
RMS_EPS = 1e-6
LN_EPS = 1e-5
DEPTH = 1
DEEPNORM_ALPHA = (2.0 * DEPTH) ** 0.25

QK_PAD_DIM = 256
VT_ROWS = V_HEAD_DIM + 16
POOL_HALO = 16
VMEM_LIMIT = 56 * 1024 * 1024

F32 = jnp.float32
BF16 = jnp.bfloat16
_NT_DIMS = (((1,), (1,)), ((), ()))


def _const_spec(shape):
    nd = len(shape)
    return pl.BlockSpec(shape, lambda *_: (0,) * nd, pipeline_mode=pl.Buffered(1))


def _params(sem):
    return pltpu.CompilerParams(dimension_semantics=sem, vmem_limit_bytes=VMEM_LIMIT)


def _mod_kernel(c_ref, w_ref, b_ref, o_ref):
    c = c_ref[...]
    sc = c * jax.nn.sigmoid(c)
    o_ref[...] = jnp.sum(sc * w_ref[...], axis=0, keepdims=True) + b_ref[...]


def _mod(c_col, w_ada, b_ada, tn=1024):
    d, n = w_ada.shape
    return pl.pallas_call(
        _mod_kernel,
        grid=(n // tn,),
        in_specs=[
            pl.BlockSpec((d, 1), lambda j: (0, 0)),
            pl.BlockSpec((d, tn), lambda j: (0, j)),
            pl.BlockSpec((1, tn), lambda j: (0, j)),
        ],
        out_specs=pl.BlockSpec((1, tn), lambda j: (0, j)),
        out_shape=jax.ShapeDtypeStruct((1, n), F32),
        compiler_params=_params(("arbitrary",)),
        name="mod",
    )(c_col, w_ada, b_ada)


def _ln_kernel(x_ref, mod_ref, h_ref, *, d):
    x = x_ref[...]
    mu = jnp.mean(x, axis=-1, keepdims=True)
    xc = x - mu
    var = jnp.mean(xc * xc, axis=-1, keepdims=True)
    y = xc * lax.rsqrt(var + LN_EPS)
    shift = mod_ref[:, 0:d]
    scale = mod_ref[:, d:2 * d]
    h_ref[...] = (y * (1.0 + scale) + shift).astype(h_ref.dtype)


def _ln(x2, mod, tm=1024):
    s, d = x2.shape
    return pl.pallas_call(
        functools.partial(_ln_kernel, d=d),
        grid=(s // tm,),
        in_specs=[
            pl.BlockSpec((tm, d), lambda i: (i, 0)),
            pl.BlockSpec((1, 3 * d), lambda i: (0, 0)),
        ],
        out_specs=pl.BlockSpec((tm, d), lambda i: (i, 0)),
        out_shape=jax.ShapeDtypeStruct((s, d), BF16),
        compiler_params=_params(("parallel",)),
        name="ln",
    )(x2, mod)


def _rms(xf, g):
    return xf * lax.rsqrt(jnp.mean(xf * xf, axis=-1, keepdims=True) + RMS_EPS) * g


def _lat_kernel(h_ref, wlat_ref, qg_ref, kvg_ref, wqt_ref, wk_ref, wvt_ref, pos_ref, freq_ref,
                qt_ref, k_ref, vt_ref, *, q_scale):
    h = h_ref[...]
    lat = lax.dot_general(h, wlat_ref[...], _NT_DIMS, preferred_element_type=F32)
    qn = _rms(lat[:, 0:Q_LORA_RANK], qg_ref[...])
    cn = _rms(lat[:, Q_LORA_RANK:Q_LORA_RANK + KV_LORA_RANK], kvg_ref[...])
    kpe = lat[:, Q_LORA_RANK + KV_LORA_RANK:]
    qn_t = qn.T.astype(BF16)
    cn_t = cn.T.astype(BF16)
    cn = cn.astype(BF16)

    ang = pos_ref[...].astype(F32) * freq_ref[...]
    lane = lax.broadcasted_iota(jnp.int32, ang.shape, 1)
    cs = jnp.where(lane < 64, jnp.cos(ang),
                   jnp.where(lane < 96, -jnp.sin(ang), jnp.sin(ang)))
    cs_t = cs.T

    r = kpe * cs
    k_rot = jnp.where(lane < 64, r + pltpu.roll(r, 64, 1), 0.0).astype(BF16)
    zeros_t = jnp.zeros((QK_PAD_DIM - QK_HEAD_DIM, qn_t.shape[1]), BF16)
    ones_t = jnp.ones((VT_ROWS - V_HEAD_DIM, qn_t.shape[1]), BF16)
    for hd in range(N_HEADS):
        qh = jnp.dot(wqt_ref[hd], qn_t, preferred_element_type=F32)
        rq = qh[128:256, :] * cs_t
        qt_ref[hd, 0:128, :] = (qh[0:128, :] * q_scale).astype(BF16)
        qt_ref[hd, 128:192, :] = ((rq[0:64, :] + rq[64:128, :]) * q_scale).astype(BF16)
        qt_ref[hd, 192:256, :] = zeros_t
    for hp in range(N_HEADS // 2):
        kk = jnp.dot(cn, wk_ref[:, hp * 256:(hp + 1) * 256], preferred_element_type=F32)
        vv = jnp.dot(wvt_ref[hp * 256:(hp + 1) * 256, :], cn_t, preferred_element_type=F32)
        for e in range(2):
            hd = 2 * hp + e
            k_ref[hd, :, 0:128] = kk[:, e * 128:(e + 1) * 128].astype(BF16)
            k_ref[hd, :, 128:256] = k_rot
            vt_ref[hd, 0:V_HEAD_DIM, :] = vv[e * 128:(e + 1) * 128, :].astype(BF16)
            vt_ref[hd, V_HEAD_DIM:VT_ROWS, :] = ones_t


def _lat(h, wlat, qg, kvg, wqt, wk, wvt, pos_col, freq, tm=512):
    s, d = h.shape
    q_scale = (QK_HEAD_DIM ** -0.5) * math.log2(math.e)
    return pl.pallas_call(
        functools.partial(_lat_kernel, q_scale=q_scale),
        grid=(s // tm,),
        in_specs=[
            pl.BlockSpec((tm, d), lambda i: (i, 0)),
            _const_spec(wlat.shape),
            _const_spec(qg.shape),
            _const_spec(kvg.shape),
            _const_spec(wqt.shape),
            _const_spec(wk.shape),
            _const_spec(wvt.shape),
            pl.BlockSpec((tm, 1), lambda i: (i, 0)),
            _const_spec(freq.shape),
        ],
        out_specs=[
            pl.BlockSpec((N_HEADS, QK_PAD_DIM, tm), lambda i: (0, 0, i)),
            pl.BlockSpec((N_HEADS, tm, QK_PAD_DIM), lambda i: (0, i, 0)),
            pl.BlockSpec((N_HEADS, VT_ROWS, tm), lambda i: (0, 0, i)),
        ],
        out_shape=[
            jax.ShapeDtypeStruct((N_HEADS, QK_PAD_DIM, s), BF16),
            jax.ShapeDtypeStruct((N_HEADS, s, QK_PAD_DIM), BF16),
            jax.ShapeDtypeStruct((N_HEADS, VT_ROWS, s), BF16),
        ],
        compiler_params=_params(("parallel",)),
        name="lat",
    )(h, wlat, qg, kvg, wqt, wk, wvt, pos_col, freq)


def _gates_kernel(h_ref, w_ref, b_ref, o_ref, *, tn, cn, d, pool_w):
    j = pl.program_id(1)
    for c in range(tn // cn):
        h = h_ref[...]
        cols = slice(c * cn, (c + 1) * cn)
        col = j * tn + c * cn
        is_ident = jnp.logical_and(col >= d, col < d + pool_w)
        is_sigm = col >= d + 2 * pool_w
        z = lax.dot_general(h, w_ref[cols, :], _NT_DIMS, preferred_element_type=F32)
        sg = jax.nn.sigmoid(z + b_ref[:, cols])
        o_ref[:, cols] = jnp.where(is_ident, z, jnp.where(is_sigm, sg, z * sg)).astype(o_ref.dtype)


def _gates(h, wg_t, bg, tm=1024, tn=2048, cn=512):
    s, d = h.shape
    n = wg_t.shape[0]
    return pl.pallas_call(
        functools.partial(_gates_kernel, tn=tn, cn=cn, d=d, pool_w=POOL_WIDTH),
        grid=(s // tm, n // tn),
        in_specs=[
            pl.BlockSpec((tm, d), lambda i, j: (i, 0)),
            pl.BlockSpec((tn, d), lambda i, j: (j, 0)),
            pl.BlockSpec((1, tn), lambda i, j: (0, j)),
        ],
        out_specs=pl.BlockSpec((tm, tn), lambda i, j: (i, j)),
        out_shape=jax.ShapeDtypeStruct((s, n), BF16),
        compiler_params=_params(("parallel", "arbitrary")),
        name="gates",
    )(h, wg_t, bg)


def _attn_kernel(qt_ref, k_ref, vt_ref, sg_ref, o_ref, sa_sc, sb_sc, m_sc, acc_sc,
                 *, bq, bk, cw, nq):
    qi = pl.program_id(1)
    q_cur = pl.multiple_of(qi * bq, bq)
    q_nxt = pl.multiple_of(jnp.minimum(qi + 1, nq - 1) * bq, bq)
    m_sc[...] = jnp.full(m_sc.shape, -jnp.inf, F32)
    acc_sc[...] = jnp.zeros(acc_sc.shape, F32)

    def qk(q_start, j, s_sc, first_col=0):
        start = pl.multiple_of(j * bk, bk)
        q_t = qt_ref[0, :, pl.ds(pl.multiple_of(q_start + first_col, cw), bq - first_col)]
        s_sc[:, first_col:bq] = jnp.dot(k_ref[0, pl.ds(start, bk), :], q_t,
                                        preferred_element_type=F32)

    def softmax_pv(j, s_sc, diag):
        start = pl.multiple_of(j * bk, bk)
        for c in range(bq // cw):
            cols = slice(c * cw, (c + 1) * cw)
            if diag is None:
                nrow, masked = bk, False
            else:
                nrow = min(max((c + 1) * cw - diag * bk, 0), bk)
                masked = diag * bk + nrow - 1 > c * cw
            if nrow == 0:
                continue
            def load_scores():
                s = s_sc[0:nrow, cols]
                if masked:
                    key = diag * bk + lax.broadcasted_iota(jnp.int32, s.shape, 0)
                    qry = c * cw + lax.broadcasted_iota(jnp.int32, s.shape, 1)
                    s = jnp.where(key <= qry, s, -jnp.inf)
                return s

            m_prev = m_sc[:, cols]
            m_new = jnp.maximum(m_prev, jnp.max(load_scores(), axis=0, keepdims=True))
            alpha = jnp.exp2(m_prev - m_new)
            m_sc[:, cols] = m_new
            p = jnp.exp2(load_scores() - m_new[0:1, :])
            vt = vt_ref[0, :, pl.ds(start, nrow)]
            pv = jnp.dot(vt, p.astype(BF16), preferred_element_type=F32)
            acc_sc[:, cols] = alpha[0:1, :] * acc_sc[:, cols] + pv

    @pl.when(qi == 0)
    def _():
        qk(q_cur, 0, sa_sc)

    def pair(a):
        qk(q_cur, a + 1, sb_sc)
        softmax_pv(a, sa_sc, None)
        qk(q_cur, a + 2, sa_sc)
        softmax_pv(a + 1, sb_sc, None)

    def four_pairs(t, carry):
        for u in range(4):
            pair(8 * t + 2 * u)
        return carry

    def two_pairs(t, carry):
        pair(4 * t)
        pair(4 * t + 2)
        return carry

    def one_pair(t, carry):
        pair(2 * t)
        return carry

    n_quad = lax.shift_right_logical(qi, 2)
    n_duo = lax.shift_right_logical(qi, 1)
    lax.fori_loop(0, n_quad, four_pairs, 0)
    lax.fori_loop(2 * n_quad, n_duo, two_pairs, 0)
    lax.fori_loop(2 * n_duo, qi, one_pair, 0)

    a = 2 * qi
    qk(q_cur, a + 1, sb_sc, first_col=bk)
    softmax_pv(a, sa_sc, 0)
    qk(q_nxt, 0, sa_sc)
    softmax_pv(a + 1, sb_sc, 1)

    for c in range(bq // cw):
        cols = slice(c * cw, (c + 1) * cw)
        o_t = acc_sc[0:V_HEAD_DIM, cols] / acc_sc[V_HEAD_DIM:V_HEAD_DIM + 1, cols]
        o_ref[cols, :] = (o_t.T * sg_ref[cols, :].astype(F32)).astype(o_ref.dtype)


def _attn(qt, k, vt, g, bq=1024, cw=256):
    nh, _, s = qt.shape
    bk = bq // 2
    nq = s // bq
    return pl.pallas_call(
        functools.partial(_attn_kernel, bq=bq, bk=bk, cw=cw, nq=nq),
        grid=(nh, nq),
        in_specs=[
            pl.BlockSpec((1, QK_PAD_DIM, s), lambda h, i: (h, 0, 0)),
            pl.BlockSpec((1, s, QK_PAD_DIM), lambda h, i: (h, 0, 0)),
            pl.BlockSpec((1, VT_ROWS, s), lambda h, i: (h, 0, 0)),
            pl.BlockSpec((bq, V_HEAD_DIM), lambda h, i: (i, h)),
        ],
        out_specs=pl.BlockSpec((bq, V_HEAD_DIM), lambda h, i: (i, h)),
        out_shape=jax.ShapeDtypeStruct((s, nh * V_HEAD_DIM), BF16),
        scratch_shapes=[
            pltpu.VMEM((bk, bq), F32),
            pltpu.VMEM((bk, bq), F32),
            pltpu.VMEM((8, bq), F32),
            pltpu.VMEM((VT_ROWS, bq), F32),
        ],
        compiler_params=_params(("arbitrary", "arbitrary")),
        name="attn",
    )(qt, k, vt, g)


def _pool_mix(u_ref, halo_ref, pg_ref, w_ref, sc_ref, i, tm):
    u = u_ref[...].astype(F32)
    halo = jnp.where(i > 0, halo_ref[...].astype(F32), 0.0)
    ext = jnp.concatenate([halo, u], axis=0)
    t = i * tm + lax.broadcasted_iota(jnp.int32, (tm, 1), 0)
    n_ext = POOL_HALO + tm

    def shifted(a, kk):
        return jnp.concatenate([jnp.zeros((kk, a.shape[1]), F32), a[:a.shape[0] - kk]], axis=0)

    acc = ext
    width = 1
    outs = []
    for g, w in enumerate(POOL_WINDOWS):
        lo = g * POOL_GROUP_DIM
        acc = acc[:, (0 if g == 0 else POOL_GROUP_DIM):]
        while width < w:
            acc = acc + shifted(acc, width)
            width *= 2
        win = acc[POOL_HALO:n_ext, 0:POOL_GROUP_DIM]
        cnt = jnp.minimum(t + 1, w).astype(F32)
        pooled = win / cnt - u[:, lo:lo + POOL_GROUP_DIM]
        mixed = jnp.dot(pooled.astype(BF16), w_ref[g], preferred_element_type=F32)
        mixed = mixed * sc_ref[:, lo:lo + POOL_GROUP_DIM]
        outs.append((mixed * pg_ref[:, lo:lo + POOL_GROUP_DIM].astype(F32)).astype(BF16))
    return jnp.concatenate(outs, axis=1)


def _final_kernel(ya_ref, u_ref, halo_ref, pg_ref, gm_ref, gp_ref, x_ref, mod_ref, lng_ref,
                  lnb_ref, wpg_ref, psc_ref, wmo_ref, wpo_ref, wout_ref, o_ref, *, d, tm):
    yp = _pool_mix(u_ref, halo_ref, pg_ref, wpg_ref, psc_ref, pl.program_id(0), tm)
    y_mla = jnp.dot(ya_ref[...], wmo_ref[...], preferred_element_type=F32)
    y_pool = jnp.dot(yp, wpo_ref[...], preferred_element_type=F32)
    merged = gm_ref[...].astype(F32) * y_mla + gp_ref[...].astype(F32) * y_pool
    y = jnp.dot(merged.astype(BF16), wout_ref[...], preferred_element_type=F32)
    gate = mod_ref[:, 2 * d:3 * d]
    r = DEEPNORM_ALPHA * x_ref[...] + (1.0 + gate) * y
    mu = jnp.mean(r, axis=-1, keepdims=True)
    rc = r - mu
    var = jnp.mean(rc * rc, axis=-1, keepdims=True)
    o_ref[...] = rc * lax.rsqrt(var + LN_EPS) * lng_ref[...] + lnb_ref[...]


def _final(ya, g, x2, mod, ln_g, ln_b, w_pool, pool_scale, wmo, wpo, wout, tm=256):
    s, d = x2.shape
    p = POOL_WIDTH
    gb = (d + 2 * p) // d
    ub = d // p
    hb = tm // POOL_HALO
    return pl.pallas_call(
        functools.partial(_final_kernel, d=d, tm=tm),
        grid=(s // tm,),
        in_specs=[
            pl.BlockSpec((tm, ya.shape[1]), lambda i: (i, 0)),
            pl.BlockSpec((tm, p), lambda i: (i, ub)),
            pl.BlockSpec((POOL_HALO, p), lambda i: (jnp.maximum(i * hb - 1, 0), ub)),
            pl.BlockSpec((tm, p), lambda i: (i, ub + 1)),
            pl.BlockSpec((tm, d), lambda i: (i, gb)),
            pl.BlockSpec((tm, d), lambda i: (i, gb + 1)),
            pl.BlockSpec((tm, d), lambda i: (i, 0)),
            _const_spec(mod.shape),
            _const_spec(ln_g.shape),
            _const_spec(ln_b.shape),
            _const_spec(w_pool.shape),
            _const_spec(pool_scale.shape),
            _const_spec(wmo.shape),
            _const_spec(wpo.shape),
            _const_spec(wout.shape),
        ],
        out_specs=pl.BlockSpec((tm, d), lambda i: (i, 0)),
        out_shape=jax.ShapeDtypeStruct((s, d), F32),
        compiler_params=_params(("parallel",)),
        name="final",
    )(ya, g, g, g, g, g, x2, mod, ln_g, ln_b, w_pool, pool_scale, wmo, wpo, wout)


def _swap_halves(w):
    half = QK_ROPE_DIM // 2
    return jnp.concatenate([w[..., half:], w[..., :half]], axis=-1)


def _prep_kernel(w_ref, wlat_ref, wg_ref):
    o_kv = Q_LORA_RANK + KV_LORA_RANK
    o_g = o_kv + QK_ROPE_DIM
    half = QK_ROPE_DIM // 2
    wlat_ref[0:o_g, :] = w_ref[0:o_g, :].astype(BF16)
    wlat_ref[o_g:o_g + half, :] = w_ref[o_kv + half:o_g, :].astype(BF16)
    wlat_ref[o_g + half:o_g + 2 * half, :] = w_ref[o_kv:o_kv + half, :].astype(BF16)
    wg_ref[...] = w_ref[o_g:, :].astype(BF16)


def _prep_w_in(w_t, tc=256):
    n, d = w_t.shape
    n_lat = Q_LORA_RANK + KV_LORA_RANK + 2 * QK_ROPE_DIM
    n_g = n - n_lat + QK_ROPE_DIM
    return pl.pallas_call(
        _prep_kernel,
        grid=(d // tc,),
        in_specs=[pl.BlockSpec((n, tc), lambda i: (0, i))],
        out_specs=[pl.BlockSpec((n_lat, tc), lambda i: (0, i)),
                   pl.BlockSpec((n_g, tc), lambda i: (0, i))],
        out_shape=[jax.ShapeDtypeStruct((n_lat, d), BF16), jax.ShapeDtypeStruct((n_g, d), BF16)],
        compiler_params=_params(("parallel",)),
        name="prep",
    )(w_t)


def kernel(x, c, positions, w_ada, b_ada, w_in, b_gates, q_norm_g, w_q_b, kv_norm_g, w_kv_b,
           w_mla_o, w_pool_g, pool_scale, w_pool_o, w_out, ln_g, ln_b):
    b, s, d = x.shape
    assert b == 1 and w_ada.shape[0] == 1
    l = 0
    x2 = x.reshape(s, d)

    wlat_t, wg_t = _prep_w_in(w_in[l].T)
    bg = jnp.concatenate([jnp.zeros((d + 2 * POOL_WIDTH,), F32), b_gates[l]]).reshape(1, -1)
    wq3 = w_q_b[l].reshape(Q_LORA_RANK, N_HEADS, QK_HEAD_DIM)
    wq_pe = wq3[..., QK_NOPE_DIM:]
    wq = jnp.concatenate([wq3[..., :QK_NOPE_DIM], wq_pe, _swap_halves(wq_pe)], axis=-1)
    wqt = jnp.transpose(wq, (1, 2, 0)).astype(BF16)
    wkv3 = w_kv_b[l].reshape(KV_LORA_RANK, N_HEADS, QK_NOPE_DIM + V_HEAD_DIM)
    wk = wkv3[..., :QK_NOPE_DIM].reshape(KV_LORA_RANK, N_HEADS * QK_NOPE_DIM).astype(BF16)
    wvt = jnp.transpose(wkv3[..., QK_NOPE_DIM:], (1, 2, 0)).reshape(
        N_HEADS * V_HEAD_DIM, KV_LORA_RANK).astype(BF16)
    inv_freq = ROPE_THETA ** (-jnp.arange(0, QK_ROPE_DIM, 2, dtype=F32) / QK_ROPE_DIM)
    freq = jnp.tile(inv_freq, 4).reshape(1, 128)
    pos_col = positions.reshape(s, 1)

    mod = _mod(c.reshape(d, 1), w_ada[l], b_ada[l].reshape(1, -1))
    h = _ln(x2, mod)
    qt, k, vt = _lat(h, wlat_t, q_norm_g[l].reshape(1, -1), kv_norm_g[l].reshape(1, -1),
                     wqt, wk, wvt, pos_col, freq)
    g = _gates(h, wg_t, bg)
    ya = _attn(qt, k, vt, g)
    out = _final(ya, g, x2, mod, ln_g[l].reshape(1, -1), ln_b[l].reshape(1, -1),
                 w_pool_g[l].astype(BF16), pool_scale[l].reshape(1, -1),
                 w_mla_o[l].astype(BF16), w_pool_o[l].astype(BF16), w_out[l].astype(BF16))
    return out.reshape(b, s, d)
```

```python
import functools
import math

import jax
import jax.numpy as jnp
from jax import lax
from jax.experimental import pallas as pl
from jax.experimental.pallas import tpu as pltpu

N_HEADS = 16
QK_NOPE_DIM = 128
QK_ROPE_DIM = 64
QK_HEAD_DIM = QK_NOPE_DIM + QK_ROPE_DIM
V_HEAD_DIM = 128
Q_LORA_RANK = 512
KV_LORA_RANK = 512
ROPE_THETA = 10000.0
POOL_WINDOWS = (2, 4, 8, 16)
POOL_GROUP_DIM = 256
POOL_WIDTH = len(POOL_WINDOWS) * POOL_GROUP_DIM
RMS_EPS = 1e-6
LN_EPS = 1e-5
DEPTH = 1
DEEPNORM_ALPHA = (2.0 * DEPTH) ** 0.25

QK_PAD_DIM = 256
VT_ROWS = V_HEAD_DIM + 16
POOL_HALO = 16
VMEM_LIMIT = 56 * 1024 * 1024

F32 = jnp.float32
BF16 = jnp.bfloat16
_NT_DIMS = (((1,), (1,)), ((), ()))


def _const_spec(shape):
    nd = len(shape)
    return pl.BlockSpec(shape, lambda *_: (0,) * nd, pipeline_mode=pl.Buffered(1))


def _params(sem):
    return pltpu.CompilerParams(dimension_semantics=sem, vmem_limit_bytes=VMEM_LIMIT)


def _mod_kernel(c_ref, w_ref, b_ref, o_ref):
    c = c_ref[...]
    sc = c * jax.nn.sigmoid(c)
    o_ref[...] = jnp.sum(sc * w_ref[...], axis=0, keepdims=True) + b_ref[...]


def _mod(c_col, w_ada, b_ada, tn=512):
    d, n = w_ada.shape
    return pl.pallas_call(
        _mod_kernel,
        grid=(n // tn,),
        in_specs=[
            pl.BlockSpec((d, 1), lambda j: (0, 0)),
            pl.BlockSpec((d, tn), lambda j: (0, j)),
            pl.BlockSpec((1, tn), lambda j: (0, j)),
        ],
        out_specs=pl.BlockSpec((1, tn), lambda j: (0, j)),
        out_shape=jax.ShapeDtypeStruct((1, n), F32),
        compiler_params=_params(("arbitrary",)),
        name="mod",
    )(c_col, w_ada, b_ada)


def _ln_kernel(x_ref, mod_ref, h_ref, *, d):
    x = x_ref[...]
    mu = jnp.mean(x, axis=-1, keepdims=True)
    xc = x - mu
    var = jnp.mean(xc * xc, axis=-1, keepdims=True)
    y = xc * lax.rsqrt(var + LN_EPS)
    shift = mod_ref[:, 0:d]
    scale = mod_ref[:, d:2 * d]
    h_ref[...] = (y * (1.0 + scale) + shift).astype(h_ref.dtype)


def _ln(x2, mod, tm=512):
    s, d = x2.shape
    return pl.pallas_call(
        functools.partial(_ln_kernel, d=d),
        grid=(s // tm,),
        in_specs=[
            pl.BlockSpec((tm, d), lambda i: (i, 0)),
            pl.BlockSpec((1, 3 * d), lambda i: (0, 0)),
        ],
        out_specs=pl.BlockSpec((tm, d), lambda i: (i, 0)),
        out_shape=jax.ShapeDtypeStruct((s, d), BF16),
        compiler_params=_params(("parallel",)),
        name="ln",
    )(x2, mod)


def _rms(xf, g):
    return xf * lax.rsqrt(jnp.mean(xf * xf, axis=-1, keepdims=True) + RMS_EPS) * g


def _lat_kernel(h_ref, wlat_ref, qg_ref, kvg_ref, wqt_ref, wk_ref, wvt_ref, pos_ref, freq_ref,
                qt_ref, k_ref, vt_ref, *, q_scale):
    h = h_ref[...]
    lat = lax.dot_general(h, wlat_ref[...], _NT_DIMS, preferred_element_type=F32)
    qn = _rms(lat[:, 0:Q_LORA_RANK], qg_ref[...])
    cn = _rms(lat[:, Q_LORA_RANK:Q_LORA_RANK + KV_LORA_RANK], kvg_ref[...])
    kpe = lat[:, Q_LORA_RANK + KV_LORA_RANK:]
    qn_t = qn.T.astype(BF16)
    cn_t = cn.T.astype(BF16)
    cn = cn.astype(BF16)

    ang = pos_ref[...].astype(F32) * freq_ref[...]
    lane = lax.broadcasted_iota(jnp.int32, ang.shape, 1)
    cs = jnp.where(lane < 64, jnp.cos(ang),
                   jnp.where(lane < 96, -jnp.sin(ang), jnp.sin(ang)))
    cs_t = cs.T

    r = kpe * cs
    k_rot = jnp.where(lane < 64, r + pltpu.roll(r, 64, 1), 0.0).astype(BF16)
    zeros_t = jnp.zeros((QK_PAD_DIM - QK_HEAD_DIM, qn_t.shape[1]), BF16)
    ones_t = jnp.ones((VT_ROWS - V_HEAD_DIM, qn_t.shape[1]), BF16)
    for hd in range(N_HEADS):
        qh = jnp.dot(wqt_ref[hd], qn_t, preferred_element_type=F32)
        rq = qh[128:256, :] * cs_t
        qt_ref[hd, 0:128, :] = (qh[0:128, :] * q_scale).astype(BF16)
        qt_ref[hd, 128:192, :] = ((rq[0:64, :] + rq[64:128, :]) * q_scale).astype(BF16)
        qt_ref[hd, 192:256, :] = zeros_t
    for hp in range(N_HEADS // 2):
        kk = jnp.dot(cn, wk_ref[:, hp * 256:(hp + 1) * 256], preferred_element_type=F32)
        vv = jnp.dot(wvt_ref[hp * 256:(hp + 1) * 256, :], cn_t, preferred_element_type=F32)
        for e in range(2):
            hd = 2 * hp + e
            k_ref[hd, :, 0:128] = kk[:, e * 128:(e + 1) * 128].astype(BF16)
            k_ref[hd, :, 128:256] = k_rot
            vt_ref[hd, 0:V_HEAD_DIM, :] = vv[e * 128:(e + 1) * 128, :].astype(BF16)
            vt_ref[hd, V_HEAD_DIM:VT_ROWS, :] = ones_t


def _lat(h, wlat, qg, kvg, wqt, wk, wvt, pos_col, freq, tm=512):
    s, d = h.shape
    q_scale = (QK_HEAD_DIM ** -0.5) * math.log2(math.e)
    return pl.pallas_call(
        functools.partial(_lat_kernel, q_scale=q_scale),
        grid=(s // tm,),
        in_specs=[
            pl.BlockSpec((tm, d), lambda i: (i, 0)),
            _const_spec(wlat.shape),
            _const_spec(qg.shape),
            _const_spec(kvg.shape),
            _const_spec(wqt.shape),
            _const_spec(wk.shape),
            _const_spec(wvt.shape),
            pl.BlockSpec((tm, 1), lambda i: (i, 0)),
            _const_spec(freq.shape),
        ],
        out_specs=[
            pl.BlockSpec((N_HEADS, QK_PAD_DIM, tm), lambda i: (0, 0, i)),
            pl.BlockSpec((N_HEADS, tm, QK_PAD_DIM), lambda i: (0, i, 0)),
            pl.BlockSpec((N_HEADS, VT_ROWS, tm), lambda i: (0, 0, i)),
        ],
        out_shape=[
            jax.ShapeDtypeStruct((N_HEADS, QK_PAD_DIM, s), BF16),
            jax.ShapeDtypeStruct((N_HEADS, s, QK_PAD_DIM), BF16),
            jax.ShapeDtypeStruct((N_HEADS, VT_ROWS, s), BF16),
        ],
        compiler_params=_params(("parallel",)),
        name="lat",
    )(h, wlat, qg, kvg, wqt, wk, wvt, pos_col, freq)


def _gates_kernel(h_ref, w_ref, b_ref, o_ref, *, tn, cn, d, pool_w):
    j = pl.program_id(1)
    h = h_ref[...]
    for c in range(tn // cn):
        cols = slice(c * cn, (c + 1) * cn)
        col = j * tn + c * cn
        is_ident = jnp.logical_and(col >= d, col < d + pool_w)
        is_sigm = col >= d + 2 * pool_w
        z = lax.dot_general(h, w_ref[cols, :], _NT_DIMS, preferred_element_type=F32)
        sg = jax.nn.sigmoid(z + b_ref[:, cols])
        o_ref[:, cols] = jnp.where(is_ident, z, jnp.where(is_sigm, sg, z * sg)).astype(o_ref.dtype)


def _gates(h, wg_t, bg, tm=1024, tn=2048, cn=512):
    s, d = h.shape
    n = wg_t.shape[0]
    return pl.pallas_call(
        functools.partial(_gates_kernel, tn=tn, cn=cn, d=d, pool_w=POOL_WIDTH),
        grid=(s // tm, n // tn),
        in_specs=[
            pl.BlockSpec((tm, d), lambda i, j: (i, 0)),
            pl.BlockSpec((tn, d), lambda i, j: (j, 0)),
            pl.BlockSpec((1, tn), lambda i, j: (0, j)),
        ],
        out_specs=pl.BlockSpec((tm, tn), lambda i, j: (i, j)),
        out_shape=jax.ShapeDtypeStruct((s, n), BF16),
        compiler_params=_params(("parallel", "arbitrary")),
        name="gates",
    )(h, wg_t, bg)


def _attn_kernel(qt_ref, k_ref, vt_ref, sg_ref, o_ref, sa_sc, sb_sc, m_sc, acc_sc,
                 *, bq, bk, cw, nq):
    qi = pl.program_id(1)
    q_cur = pl.multiple_of(qi * bq, bq)
    q_nxt = pl.multiple_of(jnp.minimum(qi + 1, nq - 1) * bq, bq)
    m_sc[...] = jnp.full(m_sc.shape, -jnp.inf, F32)
    acc_sc[...] = jnp.zeros(acc_sc.shape, F32)

    def qk(q_start, j, s_sc, first_col=0):
        start = pl.multiple_of(j * bk, bk)
        q_t = qt_ref[0, :, pl.ds(pl.multiple_of(q_start + first_col, cw), bq - first_col)]
        s_sc[:, first_col:bq] = jnp.dot(k_ref[0, pl.ds(start, bk), :], q_t,
                                        preferred_element_type=F32)

    def softmax_pv(j, s_sc, diag):
        start = pl.multiple_of(j * bk, bk)
        for c in range(bq // cw):
            cols = slice(c * cw, (c + 1) * cw)
            if diag is None:
                nrow, masked = bk, False
            else:
                nrow = min(max((c + 1) * cw - diag * bk, 0), bk)
                masked = diag * bk + nrow - 1 > c * cw
            if nrow == 0:
                continue
            def load_scores():
                s = s_sc[0:nrow, cols]
                if masked:
                    key = diag * bk + lax.broadcasted_iota(jnp.int32, s.shape, 0)
                    qry = c * cw + lax.broadcasted_iota(jnp.int32, s.shape, 1)
                    s = jnp.where(key <= qry, s, -jnp.inf)
                return s

            m_prev = m_sc[:, cols]
            m_new = jnp.maximum(m_prev, jnp.max(load_scores(), axis=0, keepdims=True))
            alpha = jnp.exp2(m_prev - m_new)
            m_sc[:, cols] = m_new
            p = jnp.exp2(load_scores() - m_new[0:1, :])
            vt = vt_ref[0, :, pl.ds(start, nrow)]
            pv = jnp.dot(vt, p.astype(BF16), preferred_element_type=F32)
            acc_sc[:, cols] = alpha[0:1, :] * acc_sc[:, cols] + pv

    @pl.when(qi == 0)
    def _():
        qk(q_cur, 0, sa_sc)

    def pair(a):
        qk(q_cur, a + 1, sb_sc)
        softmax_pv(a, sa_sc, None)
        qk(q_cur, a + 2, sa_sc)
        softmax_pv(a + 1, sb_sc, None)

    def four_pairs(t, carry):
        for u in range(4):
            pair(8 * t + 2 * u)
        return carry

    def two_pairs(t, carry):
        pair(4 * t)
        pair(4 * t + 2)
        return carry

    def one_pair(t, carry):
        pair(2 * t)
        return carry

    def eight_pairs(t, carry):
        for u in range(8):
            pair(16 * t + 2 * u)
        return carry

    n_oct = lax.shift_right_logical(qi, 3)
    n_quad = lax.shift_right_logical(qi, 2)
    n_duo = lax.shift_right_logical(qi, 1)
    lax.fori_loop(0, n_oct, eight_pairs, 0)
    lax.fori_loop(2 * n_oct, n_quad, four_pairs, 0)
    lax.fori_loop(2 * n_quad, n_duo, two_pairs, 0)
    lax.fori_loop(2 * n_duo, qi, one_pair, 0)

    a = 2 * qi
    qk(q_cur, a + 1, sb_sc, first_col=bk)
    softmax_pv(a, sa_sc, 0)
    qk(q_nxt, 0, sa_sc)
    softmax_pv(a + 1, sb_sc, 1)

    for c in range(bq // cw):
        cols = slice(c * cw, (c + 1) * cw)
        o_t = acc_sc[0:V_HEAD_DIM, cols] / acc_sc[V_HEAD_DIM:V_HEAD_DIM + 1, cols]
        o_ref[cols, :] = (o_t.T * sg_ref[cols, :].astype(F32)).astype(o_ref.dtype)


def _attn(qt, k, vt, g, bq=1024, cw=256):
    nh, _, s = qt.shape
    bk = bq // 2
    nq = s // bq
    return pl.pallas_call(
        functools.partial(_attn_kernel, bq=bq, bk=bk, cw=cw, nq=nq),
        grid=(nh, nq),
        in_specs=[
            pl.BlockSpec((1, QK_PAD_DIM, s), lambda h, i: (h, 0, 0)),
            pl.BlockSpec((1, s, QK_PAD_DIM), lambda h, i: (h, 0, 0)),
            pl.BlockSpec((1, VT_ROWS, s), lambda h, i: (h, 0, 0)),
            pl.BlockSpec((bq, V_HEAD_DIM), lambda h, i: (i, h)),
        ],
        out_specs=pl.BlockSpec((bq, V_HEAD_DIM), lambda h, i: (i, h)),
        out_shape=jax.ShapeDtypeStruct((s, nh * V_HEAD_DIM), BF16),
        scratch_shapes=[
            pltpu.VMEM((bk, bq), F32),
            pltpu.VMEM((bk, bq), F32),
            pltpu.VMEM((8, bq), F32),
            pltpu.VMEM((VT_ROWS, bq), F32),
        ],
        compiler_params=_params(("arbitrary", "arbitrary")),
        name="attn",
    )(qt, k, vt, g)


def _pool_mix(u_ref, halo_ref, pg_ref, w_ref, sc_ref, i, tm):
    u = u_ref[...].astype(F32)
    halo = jnp.where(i > 0, halo_ref[...].astype(F32), 0.0)
    ext = jnp.concatenate([halo, u], axis=0)
    t = i * tm + lax.broadcasted_iota(jnp.int32, (tm, 1), 0)
    n_ext = POOL_HALO + tm

    def shifted(a, kk):
        return jnp.concatenate([jnp.zeros((kk, a.shape[1]), F32), a[:a.shape[0] - kk]], axis=0)

    acc = ext
    width = 1
    outs = []
    for g, w in enumerate(POOL_WINDOWS):
        lo = g * POOL_GROUP_DIM
        acc = acc[:, (0 if g == 0 else POOL_GROUP_DIM):]
        while width < w:
            acc = acc + shifted(acc, width)
            width *= 2
        win = acc[POOL_HALO:n_ext, 0:POOL_GROUP_DIM]
        cnt = jnp.minimum(t + 1, w).astype(F32)
        pooled = win / cnt - u[:, lo:lo + POOL_GROUP_DIM]
        mixed = jnp.dot(pooled.astype(BF16), w_ref[g], preferred_element_type=F32)
        mixed = mixed * sc_ref[:, lo:lo + POOL_GROUP_DIM]
        outs.append((mixed * pg_ref[:, lo:lo + POOL_GROUP_DIM].astype(F32)).astype(BF16))
    return jnp.concatenate(outs, axis=1)


def _final_kernel(ya_ref, u_ref, halo_ref, pg_ref, gm_ref, gp_ref, x_ref, mod_ref, lng_ref,
                  lnb_ref, wpg_ref, psc_ref, wmo_ref, wpo_ref, wout_ref, o_ref, *, d, tm):
    yp = _pool_mix(u_ref, halo_ref, pg_ref, wpg_ref, psc_ref, pl.program_id(0), tm)
    y_mla = jnp.dot(ya_ref[...], wmo_ref[...], preferred_element_type=F32)
    y_pool = jnp.dot(yp, wpo_ref[...], preferred_element_type=F32)
    merged = gm_ref[...].astype(F32) * y_mla + gp_ref[...].astype(F32) * y_pool
    y = jnp.dot(merged.astype(BF16), wout_ref[...], preferred_element_type=F32)
    gate = mod_ref[:, 2 * d:3 * d]
    r = DEEPNORM_ALPHA * x_ref[...] + (1.0 + gate) * y
    mu = jnp.mean(r, axis=-1, keepdims=True)
    rc = r - mu
    var = jnp.mean(rc * rc, axis=-1, keepdims=True)
    o_ref[...] = rc * lax.rsqrt(var + LN_EPS) * lng_ref[...] + lnb_ref[...]


def _final(ya, g, x2, mod, ln_g, ln_b, w_pool, pool_scale, wmo, wpo, wout, tm=256):
    s, d = x2.shape
    p = POOL_WIDTH
    gb = (d + 2 * p) // d
    ub = d // p
    hb = tm // POOL_HALO
    return pl.pallas_call(
        functools.partial(_final_kernel, d=d, tm=tm),
        grid=(s // tm,),
        in_specs=[
            pl.BlockSpec((tm, ya.shape[1]), lambda i: (i, 0)),
            pl.BlockSpec((tm, p), lambda i: (i, ub)),
            pl.BlockSpec((POOL_HALO, p), lambda i: (jnp.maximum(i * hb - 1, 0), ub)),
            pl.BlockSpec((tm, p), lambda i: (i, ub + 1)),
            pl.BlockSpec((tm, d), lambda i: (i, gb)),
            pl.BlockSpec((tm, d), lambda i: (i, gb + 1)),
            pl.BlockSpec((tm, d), lambda i: (i, 0)),
            _const_spec(mod.shape),
            _const_spec(ln_g.shape),
            _const_spec(ln_b.shape),
            _const_spec(w_pool.shape),
            _const_spec(pool_scale.shape),
            _const_spec(wmo.shape),
            _const_spec(wpo.shape),
            _const_spec(wout.shape),
        ],
        out_specs=pl.BlockSpec((tm, d), lambda i: (i, 0)),
        out_shape=jax.ShapeDtypeStruct((s, d), F32),
        compiler_params=_params(("parallel",)),
        name="final",
    )(ya, g, g, g, g, g, x2, mod, ln_g, ln_b, w_pool, pool_scale, wmo, wpo, wout)


def _swap_halves(w):
    half = QK_ROPE_DIM // 2
    return jnp.concatenate([w[..., half:], w[..., :half]], axis=-1)


def _prep_kernel(w_ref, wlat_ref, wg_ref):
    o_kv = Q_LORA_RANK + KV_LORA_RANK
    o_g = o_kv + QK_ROPE_DIM
    half = QK_ROPE_DIM // 2
    wlat_ref[0:o_g, :] = w_ref[0:o_g, :].astype(BF16)
    wlat_ref[o_g:o_g + half, :] = w_ref[o_kv + half:o_g, :].astype(BF16)
    wlat_ref[o_g + half:o_g + 2 * half, :] = w_ref[o_kv:o_kv + half, :].astype(BF16)
    wg_ref[...] = w_ref[o_g:, :].astype(BF16)


def _prep_w_in(w_t, tc=256):
    n, d = w_t.shape
    n_lat = Q_LORA_RANK + KV_LORA_RANK + 2 * QK_ROPE_DIM
    n_g = n - n_lat + QK_ROPE_DIM
    return pl.pallas_call(
        _prep_kernel,
        grid=(d // tc,),
        in_specs=[pl.BlockSpec((n, tc), lambda i: (0, i))],
        out_specs=[pl.BlockSpec((n_lat, tc), lambda i: (0, i)),
                   pl.BlockSpec((n_g, tc), lambda i: (0, i))],
        out_shape=[jax.ShapeDtypeStruct((n_lat, d), BF16), jax.ShapeDtypeStruct((n_g, d), BF16)],
        compiler_params=_params(("parallel",)),
        name="prep",
    )(w_t)


def kernel(x, c, positions, w_ada, b_ada, w_in, b_gates, q_norm_g, w_q_b, kv_norm_g, w_kv_b,
           w_mla_o, w_pool_g, pool_scale, w_pool_o, w_out, ln_g, ln_b):
    b, s, d = x.shape
    assert b == 1 and w_ada.shape[0] == 1
    l = 0
    x2 = x.reshape(s, d)

    wlat_t, wg_t = _prep_w_in(w_in[l].T)
    bg = jnp.concatenate([jnp.zeros((d + 2 * POOL_WIDTH,), F32), b_gates[l]]).reshape(1, -1)
    wq3 = w_q_b[l].reshape(Q_LORA_RANK, N_HEADS, QK_HEAD_DIM)
    wq_pe = wq3[..., QK_NOPE_DIM:]
    wq = jnp.concatenate([wq3[..., :QK_NOPE_DIM], wq_pe, _swap_halves(wq_pe)], axis=-1)
    wqt = jnp.transpose(wq, (1, 2, 0)).astype(BF16)
    wkv3 = w_kv_b[l].reshape(KV_LORA_RANK, N_HEADS, QK_NOPE_DIM + V_HEAD_DIM)
    wk = wkv3[..., :QK_NOPE_DIM].reshape(KV_LORA_RANK, N_HEADS * QK_NOPE_DIM).astype(BF16)
    wvt = jnp.transpose(wkv3[..., QK_NOPE_DIM:], (1, 2, 0)).reshape(
        N_HEADS * V_HEAD_DIM, KV_LORA_RANK).astype(BF16)
    inv_freq = ROPE_THETA ** (-jnp.arange(0, QK_ROPE_DIM, 2, dtype=F32) / QK_ROPE_DIM)
    freq = jnp.tile(inv_freq, 4).reshape(1, 128)
    pos_col = positions.reshape(s, 1)

    mod = _mod(c.reshape(d, 1), w_ada[l], b_ada[l].reshape(1, -1))
    h = _ln(x2, mod)
    qt, k, vt = _lat(h, wlat_t, q_norm_g[l].reshape(1, -1), kv_norm_g[l].reshape(1, -1),
                     wqt, wk, wvt, pos_col, freq)
    g = _gates(h, wg_t, bg)
    ya = _attn(qt, k, vt, g)
    out = _final(ya, g, x2, mod, ln_g[l].reshape(1, -1), ln_b[l].reshape(1, -1),
                 w_pool_g[l].astype(BF16), pool_scale[l].reshape(1, -1),
                 w_mla_o[l].astype(BF16), w_pool_o[l].astype(BF16), w_out[l].astype(BF16))
    return out.reshape(b, s, d)
```
